```python
import jax, jax.numpy as jnp
from jax import lax
import numpy as np

D_MODEL = 1024
BATCH = 32
SEQ = 2048
DEPTH = 1

POOL_GROUPS = 4
POOL_WINDOWS = (2, 4, 8, 16)
POOL_GROUP_W = 128
POOL_W = POOL_GROUPS * POOL_GROUP_W
POOL_OUT_GROUP_W = D_MODEL // POOL_GROUPS
ATTN_HEADS = 16
HEAD_DIM = 64
ATTN_W = ATTN_HEADS * HEAD_DIM
Q_BLOCK = 128
N_BRANCH = 2
IN_W = POOL_W + 3 * ATTN_W + ATTN_HEADS + N_BRANCH * D_MODEL
PEER_HEADS = 8
N_KEYS = 128
N_EXPERTS = N_KEYS * N_KEYS
PEER_HALF = 128
PEER_KEY_DIM = 2 * PEER_HALF
PEER_TOPK = 16
TOK_CHUNK = 128
N_MOD = 6
EPS = 1e-6
NEG_INF = -1e30

kernel_name = "hybrid_pool_fox_peer_adaln_block"


def rmsnorm(x, g):
    xf = x.astype(jnp.float32)
    y = xf * lax.rsqrt(jnp.mean(xf * xf, axis=-1, keepdims=True) + EPS)
    return (y * g.astype(jnp.float32)).astype(x.dtype)


def modulate(h, shift, scale):
    return h * (1.0 + scale[:, None, :]) + shift[:, None, :]


def causal_pool_mixer(u, w_pool, pool_scale):
    B, S, _ = u.shape
    ug = u.reshape(B, S, POOL_GROUPS, POOL_GROUP_W)
    cs = jnp.cumsum(ug.astype(jnp.float32), axis=1)
    pos = jnp.arange(S)
    means = []
    for g, w in enumerate(POOL_WINDOWS):
        csg = cs[:, :, g]
        lag = jnp.pad(csg, ((0, 0), (w, 0), (0, 0)))[:, :S]
        cnt = jnp.minimum(pos + 1, w).astype(jnp.float32)[None, :, None]
        means.append((csg - lag) / cnt)
    pooled = jnp.stack(means, axis=2).astype(u.dtype) - ug
    y = jnp.einsum('bsgc,gcd->bsgd', pooled, w_pool).reshape(B, S, D_MODEL)
    return y * pool_scale


def forgetting_attention(q, k, v, log_f):
    B, S, H, Dh = q.shape
    F = jnp.transpose(jnp.cumsum(log_f.astype(jnp.float32), axis=1), (0, 2, 1))
    scale = Dh ** -0.5
    outs = []
    for i in range(S // Q_BLOCK):
        q0 = i * Q_BLOCK
        L = q0 + Q_BLOCK
        s = jnp.einsum('bqhd,bkhd->bhqk', q[:, q0:L], k[:, :L],
                       preferred_element_type=jnp.float32) * scale
        s = s + F[:, :, q0:L, None] - F[:, :, None, :L]
        qpos = q0 + jnp.arange(Q_BLOCK)
        kpos = jnp.arange(L)
        s = jnp.where(kpos[None, :] <= qpos[:, None], s, NEG_INF)
        p = jax.nn.softmax(s, axis=-1).astype(v.dtype)
        outs.append(jnp.einsum('bhqk,bkhd->bqhd', p, v[:, :L]))
    return jnp.concatenate(outs, axis=1).reshape(B, S, H * Dh)


def peer_ffn(h, w_query, sub_keys, expert_u, expert_v):
    B, S, D = h.shape
    q = jnp.einsum('bsd,dk->bsk', h, w_query).reshape(B, S, PEER_HEADS, 2, PEER_HALF)
    sc = jnp.einsum('bshpc,hpnc->bshpn', q, sub_keys,
                    preferred_element_type=jnp.float32)
    s1, i1 = lax.top_k(sc[..., 0, :], PEER_TOPK)
    s2, i2 = lax.top_k(sc[..., 1, :], PEER_TOPK)
    cand_s = (s1[..., :, None] + s2[..., None, :]).reshape(B, S, PEER_HEADS, PEER_TOPK * PEER_TOPK)
    cand_i = (i1[..., :, None] * N_KEYS + i2[..., None, :]).reshape(B, S, PEER_HEADS, PEER_TOPK * PEER_TOPK)
    top_s, sel = lax.top_k(cand_s, PEER_TOPK)
    idx = jnp.take_along_axis(cand_i, sel, axis=-1)
    gates = jax.nn.softmax(top_s, axis=-1).astype(h.dtype)
    T = B * S
    n_chunks = T // TOK_CHUNK
    hc = h.reshape(n_chunks, TOK_CHUNK, D)
    ic = idx.reshape(n_chunks, TOK_CHUNK, PEER_HEADS, PEER_TOPK)
    gc = gates.reshape(n_chunks, TOK_CHUNK, PEER_HEADS, PEER_TOPK)

    def chunk(args):
        xc, ec, wc = args
        u = expert_u[ec]
        a = jax.nn.gelu(jnp.einsum('cd,chkd->chk', xc, u), approximate=False) * wc
        return jnp.einsum('chk,chkd->cd', a, expert_v[ec])

    out = lax.map(chunk, (hc, ic, gc))
    return out.reshape(B, S, D)


def setup_inputs(seed: int = 0) -> dict:
    key = jax.random.key(seed)
    ks = jax.random.split(key, 20)
    nrm = jax.random.normal
    d = D_MODEL
    x = nrm(ks[0], (BATCH, SEQ, d), jnp.float32)
    c = nrm(ks[1], (BATCH, d), jnp.float32)
    w_mod = nrm(ks[2], (DEPTH, d, N_MOD * d), jnp.float32) * (0.5 * d ** -0.5)
    b_mod = nrm(ks[3], (DEPTH, N_MOD * d), jnp.float32) * 0.02
    norm1_g = 1.0 + 0.02 * nrm(ks[4], (DEPTH, d), jnp.float32)
    w_in = nrm(ks[5], (DEPTH, d, IN_W), jnp.float32) * d ** -0.5
    b_f = jax.random.uniform(ks[6], (DEPTH, ATTN_HEADS), jnp.float32, 1.0, 4.0)
    w_pool = nrm(ks[7], (DEPTH, POOL_GROUPS, POOL_GROUP_W, POOL_OUT_GROUP_W), jnp.float32) * POOL_GROUP_W ** -0.5
    pool_scale = 1.0 + 0.02 * nrm(ks[8], (DEPTH, d), jnp.float32)
    w_out = nrm(ks[9], (DEPTH, d, d), jnp.float32) * d ** -0.5
    norm2_g = 1.0 + 0.02 * nrm(ks[10], (DEPTH, d), jnp.float32)
    peer_w_query = nrm(ks[11], (DEPTH, d, PEER_HEADS * PEER_KEY_DIM), jnp.float32) * d ** -0.5
    peer_sub_keys = nrm(ks[12], (DEPTH, PEER_HEADS, 2, N_KEYS, PEER_HALF), jnp.float32) * PEER_HALF ** -0.5
    peer_u = nrm(ks[13], (DEPTH, N_EXPERTS, d), jnp.float32) * d ** -0.5
    peer_v = nrm(ks[14], (DEPTH, N_EXPERTS, d), jnp.float32) * PEER_HEADS ** -0.5
    final_g = 1.0 + 0.02 * nrm(ks[15], (d,), jnp.float32)
    return {"x": x, "c": c, "w_mod": w_mod, "b_mod": b_mod, "norm1_g": norm1_g,
            "w_in": w_in, "b_f": b_f, "w_pool": w_pool, "pool_scale": pool_scale,
            "w_out": w_out, "norm2_g": norm2_g, "peer_w_query": peer_w_query,
            "peer_sub_keys": peer_sub_keys, "peer_u": peer_u, "peer_v": peer_v,
            "final_g": final_g}


def reference(x, c, w_mod, b_mod, norm1_g, w_in, b_f, w_pool, pool_scale, w_out,
              norm2_g, peer_w_query, peer_sub_keys, peer_u, peer_v, final_g):
    B, S, D = x.shape
    for l in range(DEPTH):
        mod = jnp.einsum('bd,de->be', c, w_mod[l]) + b_mod[l]
        sh1, sc1, gt1, sh2, sc2, gt2 = jnp.split(mod, N_MOD, axis=-1)

        h = modulate(rmsnorm(x, norm1_g[l]), sh1, sc1)
        z = jnp.einsum('bsd,de->bse', h, w_in[l])
        o = 0
        u_pool = z[..., o:o + POOL_W]; o += POOL_W
        q = z[..., o:o + ATTN_W].reshape(B, S, ATTN_HEADS, HEAD_DIM); o += ATTN_W
        k = z[..., o:o + ATTN_W].reshape(B, S, ATTN_HEADS, HEAD_DIM); o += ATTN_W
        v = z[..., o:o + ATTN_W].reshape(B, S, ATTN_HEADS, HEAD_DIM); o += ATTN_W
        f_logit = z[..., o:o + ATTN_HEADS]; o += ATTN_HEADS
        g_a = jax.nn.sigmoid(z[..., o:o + D]); o += D
        g_b = jax.nn.sigmoid(z[..., o:o + D])

        y_a = causal_pool_mixer(u_pool, w_pool[l], pool_scale[l])
        log_f = jax.nn.log_sigmoid((f_logit + b_f[l]).astype(jnp.float32))
        y_b = forgetting_attention(q, k, v, log_f)
        y = g_a * y_a + g_b * y_b
        x = x + gt1[:, None, :] * jnp.einsum('bsd,de->bse', y, w_out[l])

        h2 = modulate(rmsnorm(x, norm2_g[l]), sh2, sc2)
        x = x + gt2[:, None, :] * peer_ffn(h2, peer_w_query[l], peer_sub_keys[l], peer_u[l], peer_v[l])
    return rmsnorm(x, final_g)
```

```python
import functools

import jax
import jax.numpy as jnp
from jax import lax
from jax.experimental import pallas as pl
from jax.experimental.pallas import tpu as pltpu

F32 = jnp.float32
BF16 = jnp.bfloat16

EPS = 1e-6
NEG_INF = -1e30

POOL_WINDOWS = (2, 4, 8, 16)
POOL_GROUP_W = 128
POOL_W = len(POOL_WINDOWS) * POOL_GROUP_W
ATTN_HEADS = 16
HEAD_DIM = 64
PEER_HEADS = 8
N_KEYS = 128
PEER_TOPK = 16
N_MOD = 6

LANES = 128
VMEM_LIMIT = 56 * 1024 * 1024

INPROJ_TM = 512
ATTN_TQ = 256
ATTN_TK = 256
MIX_TM = 256
PEER_TM = 512
PEER_EB = 1024


def _params(sem):
    return pltpu.CompilerParams(dimension_semantics=sem, vmem_limit_bytes=VMEM_LIMIT)


def _split3(a):
    hi = a.astype(BF16)
    r1 = a - hi.astype(F32)
    mid = r1.astype(BF16)
    lo = (r1 - mid.astype(F32)).astype(BF16)
    return hi, mid, lo


def _dot(a, b):
    return jnp.dot(a, b, preferred_element_type=F32)


def _dot_nt(a, b):
    return lax.dot_general(a, b, (((1,), (1,)), ((), ())), preferred_element_type=F32)


def _rms_mod(x, g, shift, scale):
    ms = jnp.mean(x * x, axis=-1, keepdims=True)
    y = x * lax.rsqrt(ms + EPS) * g
    return y * (1.0 + scale) + shift


def _mod_kernel(c_ref, w_ref, b_ref, o_ref):
    c_hi, c_mid, _ = _split3(c_ref[...])
    w_hi, w_mid, _ = _split3(w_ref[...])
    acc = _dot(c_hi, w_hi) + _dot(c_hi, w_mid) + _dot(c_mid, w_hi)
    o_ref[...] = acc + b_ref[...]


def _mod_call(c, w_mod, b_mod):
    bsz, d = c.shape
    n = w_mod.shape[1]
    bn = 1024
    return pl.pallas_call(
        _mod_kernel,
        grid=(n // bn,),
        in_specs=[pl.BlockSpec((bsz, d), lambda j: (0, 0)),
                  pl.BlockSpec((d, bn), lambda j: (0, j)),
                  pl.BlockSpec((1, bn), lambda j: (0, j))],
        out_specs=pl.BlockSpec((bsz, bn), lambda j: (0, j)),
        out_shape=jax.ShapeDtypeStruct((bsz, n), F32),
        compiler_params=_params(("parallel",)),
        name="mod",
    )(c, w_mod, b_mod.reshape(1, n))


def _inproj_kernel(x_ref, mod_ref, g_ref, w_ref, wf_ref, bf_ref,
                   u_ref, q_ref, k_ref, v_ref, ga_ref, gb_ref, ft_ref,
                   carry_ref, *, tiles_per_seq, d_model):
    i = pl.program_id(0)
    tm = x_ref.shape[0]
    h = _rms_mod(x_ref[...], g_ref[...], mod_ref[0, 0:1, :], mod_ref[0, 1:2, :])
    hb = h.astype(BF16)

    cw = 512

    def proj(c0):
        return _dot(hb, w_ref[:, c0:c0 + cw])

    u_ref[...] = proj(0).astype(BF16)
    base = POOL_W
    scale = HEAD_DIM ** -0.5
    for c in range(d_model // cw):
        q_ref[:, c * cw:(c + 1) * cw] = (proj(base + c * cw) * scale).astype(BF16)
    base += d_model
    for c in range(d_model // cw):
        k_ref[:, c * cw:(c + 1) * cw] = proj(base + c * cw).astype(BF16)
    base += d_model
    for c in range(d_model // cw):
        v_ref[:, c * cw:(c + 1) * cw] = proj(base + c * cw).astype(BF16)
    base += d_model
    for c in range(d_model // cw):
        ga_ref[:, c * cw:(c + 1) * cw] = jax.nn.sigmoid(proj(base + c * cw)).astype(BF16)
    base += d_model
    for c in range(d_model // cw):
        gb_ref[:, c * cw:(c + 1) * cw] = jax.nn.sigmoid(proj(base + c * cw)).astype(BF16)

    zf = _dot_nt(wf_ref[...], hb) + bf_ref[...]
    logf = jnp.minimum(zf, 0.0) - jnp.log1p(jnp.exp(-jnp.abs(zf)))
    row = lax.broadcasted_iota(jnp.int32, (tm, tm), 0)
    col = lax.broadcasted_iota(jnp.int32, (tm, tm), 1)
    tri = jnp.where(row <= col, 1.0, 0.0).astype(BF16)
    hi, mid, lo = _split3(logf)
    cs = _dot(hi, tri) + _dot(mid, tri) + _dot(lo, tri)

    @pl.when(i % tiles_per_seq == 0)
    def _():
        carry_ref[...] = jnp.zeros_like(carry_ref)

    f_cum = cs + carry_ref[:, 0:1]
    ft_ref[...] = f_cum
    carry_ref[...] = jnp.broadcast_to(f_cum[:, tm - 1:tm], carry_ref.shape)


def _inproj_call(x2d, mod3, norm_g, w_main, w_ft, b_f, seq):
    t, d = x2d.shape
    tm = INPROJ_TM
    tps = seq // tm
    nw = w_main.shape[1]
    row = lambda i: (i, 0)
    const = lambda i: (0, 0)
    out_shape = (
        jax.ShapeDtypeStruct((t, POOL_W), BF16),
        jax.ShapeDtypeStruct((t, d), BF16), jax.ShapeDtypeStruct((t, d), BF16),
        jax.ShapeDtypeStruct((t, d), BF16), jax.ShapeDtypeStruct((t, d), BF16),
        jax.ShapeDtypeStruct((t, d), BF16),
        jax.ShapeDtypeStruct((ATTN_HEADS, t), F32),
    )
    out_specs = (
        pl.BlockSpec((tm, POOL_W), row),
        pl.BlockSpec((tm, d), row), pl.BlockSpec((tm, d), row), pl.BlockSpec((tm, d), row),
        pl.BlockSpec((tm, d), row), pl.BlockSpec((tm, d), row),
        pl.BlockSpec((ATTN_HEADS, tm), lambda i: (0, i)),
    )
    return pl.pallas_call(
        functools.partial(_inproj_kernel, tiles_per_seq=tps, d_model=d),
        grid=(t // tm,),
        in_specs=[pl.BlockSpec((tm, d), row),
                  pl.BlockSpec((1, N_MOD, d), lambda i: (i // tps, 0, 0)),
                  pl.BlockSpec((1, d), const),
                  pl.BlockSpec((d, nw), const),
                  pl.BlockSpec((ATTN_HEADS, d), const),
                  pl.BlockSpec((ATTN_HEADS, 1), const)],
        out_specs=out_specs,
        out_shape=out_shape,
        scratch_shapes=[pltpu.VMEM((ATTN_HEADS, LANES), F32)],
        compiler_params=_params(("arbitrary",)),
        name="inproj",
    )(x2d, mod3, norm_g, w_main, w_ft, b_f)


def _attn_kernel(q_ref, k_ref, v_ref, ft_ref, o_ref, m_ref, l_ref, acc_ref, *, tq, tk):
    i = pl.program_id(2)
    lane = lax.broadcasted_iota(jnp.int32, (1, LANES), 1)
    q = q_ref[...]
    zero = jnp.zeros_like(q)
    qh = (jnp.where(lane < HEAD_DIM, q, zero), jnp.where(lane >= HEAD_DIM, q, zero))
    m_ref[...] = jnp.full(m_ref.shape, NEG_INF, F32)
    l_ref[...] = jnp.zeros(l_ref.shape, F32)
    acc_ref[...] = jnp.zeros(acc_ref.shape, F32)
    row = lax.broadcasted_iota(jnp.int32, (tq, tk), 0)
    col = lax.broadcasted_iota(jnp.int32, (tq, tk), 1)

    def step(j, diag):
        start = pl.multiple_of(j * tk, tk)
        kb = k_ref[pl.ds(start, tk), :]
        vb = v_ref[pl.ds(start, tk), :]
        for h in range(2):
            s = _dot_nt(qh[h], kb) - ft_ref[0, h:h + 1, pl.ds(start, tk)]
            if diag is not None:
                s = jnp.where(col + diag * tk <= row, s, NEG_INF)
            m_prev = m_ref[h]
            m_new = jnp.maximum(m_prev, jnp.max(s, axis=-1, keepdims=True))
            alpha = jnp.exp(m_prev - m_new)
            p = jnp.exp(s - m_new)
            l_ref[h] = alpha * l_ref[h] + jnp.sum(p, axis=-1, keepdims=True)
            acc_ref[h] = alpha * acc_ref[h] + _dot(p.astype(BF16), vb)
            m_ref[h] = m_new

    n_full = i * (tq // tk)

    def body(j, carry):
        step(j, None)
        return carry

    lax.fori_loop(0, n_full, body, 0)
    for dblk in range(tq // tk):
        step(n_full + dblk, dblk)
    out = jnp.where(lane < HEAD_DIM, acc_ref[0] / l_ref[0], acc_ref[1] / l_ref[1])
    o_ref[...] = out.astype(o_ref.dtype)


def _attn_call(q, k, v, ft3, bsz, seq):
    t, d = q.shape
    tq, tk = ATTN_TQ, ATTN_TK
    nq = seq // tq
    n_pairs = d // LANES
    return pl.pallas_call(
        functools.partial(_attn_kernel, tq=tq, tk=tk),
        grid=(bsz, n_pairs, nq),
        in_specs=[pl.BlockSpec((tq, LANES), lambda b, p, i: (b * nq + i, p)),
                  pl.BlockSpec((seq, LANES), lambda b, p, i: (b, p)),
                  pl.BlockSpec((seq, LANES), lambda b, p, i: (b, p)),
                  pl.BlockSpec((1, 2, seq), lambda b, p, i: (p, 0, b))],
        out_specs=pl.BlockSpec((tq, LANES), lambda b, p, i: (b * nq + i, p)),
        out_shape=jax.ShapeDtypeStruct((t, d), BF16),
        scratch_shapes=[pltpu.VMEM((2, tq, 1), F32), pltpu.VMEM((2, tq, 1), F32),
                        pltpu.VMEM((2, tq, LANES), F32)],
        compiler_params=_params(("parallel", "parallel", "arbitrary")),
        name="attn",
    )(q, k, v, ft3)


def _mix_kernel(uc_ref, up_ref, ga_ref, gb_ref, yb_ref, x_ref, mod_ref, wp_ref, ps_ref,
                wo_ref, g2_ref, wq_ref, sk_ref,
                x1_ref, h2_ref, sc_ref, y_ref, *, tiles_per_seq):
    i = pl.program_id(0)
    tm = x_ref.shape[0]
    halo = LANES
    first = (i % tiles_per_seq) == 0
    pos0 = (i % tiles_per_seq) * tm
    r_d = lax.broadcasted_iota(jnp.int32, (tm, tm), 0)
    c_d = lax.broadcasted_iota(jnp.int32, (tm, tm), 1)
    r_o = lax.broadcasted_iota(jnp.int32, (tm, halo), 0)
    c_o = lax.broadcasted_iota(jnp.int32, (tm, halo), 1)
    pos = pos0 + lax.broadcasted_iota(jnp.int32, (tm, 1), 0)
    gw = wp_ref.shape[2]
    for g, w in enumerate(POOL_WINDOWS):
        lo, hi = g * POOL_GROUP_W, (g + 1) * POOL_GROUP_W
        u_cur = uc_ref[:, lo:hi]
        u_prev = up_ref[tm - halo:tm, lo:hi]
        lag = r_d - c_d
        band_d = jnp.where((lag >= 0) & (lag < w), 1.0, 0.0).astype(BF16)
        band_o = jnp.where((r_o + halo - c_o < w) & jnp.logical_not(first), 1.0, 0.0).astype(BF16)
        wsum = _dot(band_d, u_cur) + _dot(band_o, u_prev)
        cnt = jnp.minimum(pos + 1, w).astype(F32)
        pooled = wsum / cnt - u_cur.astype(F32)
        ya = _dot(pooled.astype(BF16), wp_ref[g]) * ps_ref[:, g * gw:(g + 1) * gw]
        sl = slice(g * gw, (g + 1) * gw)
        y = ga_ref[:, sl].astype(F32) * ya + gb_ref[:, sl].astype(F32) * yb_ref[:, sl].astype(F32)
        y_ref[:, sl] = y.astype(BF16)

    o = _dot(y_ref[...], wo_ref[...])
    x1 = x_ref[...] + mod_ref[0, 2:3, :] * o
    x1_ref[...] = x1
    h2 = _rms_mod(x1, g2_ref[...], mod_ref[0, 3:4, :], mod_ref[0, 4:5, :]).astype(BF16)
    h2_ref[...] = h2
    n_hp = sk_ref.shape[0]
    half = sk_ref.shape[2]
    for hp in range(n_hp):
        qp = _dot(h2, wq_ref[:, hp * half:(hp + 1) * half]).astype(BF16)
        sc_ref[hp] = _dot_nt(sk_ref[hp], qp)


def _mix_call(u, ga, gb, yb, x2d, mod3, w_pool, pool_scale, w_out, norm2_g, w_query, sub_keys, seq):
    t, d = x2d.shape
    tm = MIX_TM
    tps = seq // tm
    row = lambda i: (i, 0)
    c2 = lambda i: (0, 0)
    c3 = lambda i: (0, 0, 0)
    n_hp = sub_keys.shape[0]
    return pl.pallas_call(
        functools.partial(_mix_kernel, tiles_per_seq=tps),
        grid=(t // tm,),
        in_specs=[pl.BlockSpec((tm, POOL_W), row),
                  pl.BlockSpec((tm, POOL_W), lambda i: (jnp.maximum(i - 1, 0), 0)),
                  pl.BlockSpec((tm, d), row), pl.BlockSpec((tm, d), row), pl.BlockSpec((tm, d), row),
                  pl.BlockSpec((tm, d), row),
                  pl.BlockSpec((1, N_MOD, d), lambda i: (i // tps, 0, 0)),
                  pl.BlockSpec(w_pool.shape, c3),
                  pl.BlockSpec((1, d), c2),
                  pl.BlockSpec((d, d), c2),
                  pl.BlockSpec((1, d), c2),
                  pl.BlockSpec(w_query.shape, c2),
                  pl.BlockSpec(sub_keys.shape, c3)],
        out_specs=(pl.BlockSpec((tm, d), row), pl.BlockSpec((tm, d), row),
                   pl.BlockSpec((n_hp, N_KEYS, tm), lambda i: (0, 0, i))),
        out_shape=(jax.ShapeDtypeStruct((t, d), F32), jax.ShapeDtypeStruct((t, d), BF16),
                   jax.ShapeDtypeStruct((n_hp, N_KEYS, t), F32)),
        scratch_shapes=[pltpu.VMEM((tm, d), BF16)],
        compiler_params=_params(("parallel",)),
        name="mix",
    )(u, u, ga, gb, yb, x2d, mod3, w_pool, pool_scale, w_out, norm2_g, w_query, sub_keys)


def _top16_rows(s):
    cur = s
    rows = []
    for _ in range(PEER_TOPK):
        m = jnp.max(cur, axis=0, keepdims=True)
        rows.append(m)
        cur = jnp.where(cur == m, -jnp.inf, cur)
    return rows


def _stack_rows(rows, tm):
    n = len(rows)
    rid = lax.broadcasted_iota(jnp.int32, (n, tm), 0)
    arr = jnp.zeros((n, tm), F32)
    for r, v in enumerate(rows):
        arr = jnp.where(rid == r, v, arr)
    return arr


def _candidates(rows1, arr1_hi, rows2, arr2, combine):
    pieces = [combine(rows1[0], arr2)]
    for a in range(1, 8):
        pieces.append(combine(rows1[a], arr2[0:8]))
    pieces.append(combine(arr1_hi, rows2[0]))
    return jnp.concatenate(pieces, axis=0)


def _peer_stats(sc_ref, e1_ref, e2_ref, pthr_ref, h):
    tm = sc_ref.shape[2]
    s1 = sc_ref[2 * h]
    s2 = sc_ref[2 * h + 1]
    t1 = _top16_rows(s1)
    t2 = _top16_rows(s2)
    t1_hi = _stack_rows(t1[8:], tm)
    t2_arr = _stack_rows(t2, tm)
    cand = _candidates(t1, t1_hi, t2, t2_arr, lambda a, b: a + b)
    n_c = cand.shape[0]
    cid = lax.broadcasted_iota(jnp.int32, (n_c, tm), 0)
    cur = cand
    z = jnp.zeros((1, tm), F32)
    m0 = None
    for r in range(PEER_TOPK):
        m = jnp.max(cur, axis=0, keepdims=True)
        if r == 0:
            m0 = m
        first = jnp.min(jnp.where(cur == m, cid, n_c), axis=0, keepdims=True)
        cur = jnp.where(cid == first, -jnp.inf, cur)
        z = z + jnp.exp(m - m0)
    selected = cur == -jnp.inf
    inv_z = 1.0 / z
    e1_rows = [jnp.exp(v - t1[0]) * inv_z for v in t1]
    e2_rows = [jnp.exp(v - t2[0]) for v in t2]
    e1_hi = _stack_rows(e1_rows[8:], tm)
    e2_arr = _stack_rows(e2_rows, tm)
    p_cand = _candidates(e1_rows, e1_hi, e2_rows, e2_arr, lambda a, b: a * b)
    pthr = jnp.min(jnp.where(selected, p_cand, jnp.inf), axis=0, keepdims=True)
    e1_ref[h] = jnp.exp(s1 - t1[0]) * inv_z
    e2_ref[h] = jnp.exp(s2 - t2[0])
    pthr_ref[pl.ds(h, 1), :] = pthr


def _peer_kernel(sc_ref, h2_ref, u_ref, vt_ref, x1_ref, mod_ref, fg_ref, o_ref,
                 e1_ref, e2_ref, pthr_ref, acc_ref, a_ref, y_ref):
    j = pl.program_id(1)
    nj = pl.num_programs(1)
    n_heads = e1_ref.shape[0]
    rows_per_blk = u_ref.shape[0] // N_KEYS

    @pl.when(j == 0)
    def _():
        def head(h, carry):
            _peer_stats(sc_ref, e1_ref, e2_ref, pthr_ref, h)
            return carry
        lax.fori_loop(0, n_heads, head, 0)
        acc_ref[...] = jnp.zeros_like(acc_ref)

    a_ref[...] = _dot_nt(u_ref[...], h2_ref[...])

    def key_row(ii, carry):
        i_key = j * rows_per_blk + ii
        g = jnp.zeros((N_KEYS, a_ref.shape[1]), F32)
        for h in range(n_heads):
            p = e1_ref[h, pl.ds(i_key, 1), :] * e2_ref[h]
            g = g + jnp.where(p >= pthr_ref[h:h + 1, :], p, 0.0)
        r0 = pl.multiple_of(ii * N_KEYS, N_KEYS)
        a = a_ref[pl.ds(r0, N_KEYS), :]
        act = 0.5 * a * (1.0 + lax.erf(a * (0.5 ** 0.5)))
        y_ref[pl.ds(r0, N_KEYS), :] = (act * g).astype(BF16)
        return carry

    lax.fori_loop(0, rows_per_blk, key_row, 0)
    acc_ref[...] += _dot(vt_ref[...], y_ref[...])

    @pl.when(j == nj - 1)
    def _():
        peer = acc_ref[...].T
        x2 = x1_ref[...] + mod_ref[0, 5:6, :] * peer
        ms = jnp.mean(x2 * x2, axis=-1, keepdims=True)
        o_ref[...] = x2 * lax.rsqrt(ms + EPS) * fg_ref[...]


def _peer_call(sc, h2, u_bf, vt_bf, x1, mod3, final_g, seq):
    t, d = x1.shape
    tm, eb = PEER_TM, PEER_EB
    tps = seq // tm
    n_hp = sc.shape[0]
    n_exp = u_bf.shape[0]
    row = lambda i, j: (i, 0)
    return pl.pallas_call(
        _peer_kernel,
        grid=(t // tm, n_exp // eb),
        in_specs=[pl.BlockSpec((n_hp, N_KEYS, tm), lambda i, j: (0, 0, i)),
                  pl.BlockSpec((tm, d), row),
                  pl.BlockSpec((eb, d), lambda i, j: (j, 0)),
                  pl.BlockSpec((d, eb), lambda i, j: (0, j)),
                  pl.BlockSpec((tm, d), row),
                  pl.BlockSpec((1, N_MOD, d), lambda i, j: (i // tps, 0, 0)),
                  pl.BlockSpec((1, d), lambda i, j: (0, 0))],
        out_specs=pl.BlockSpec((tm, d), row),
        out_shape=jax.ShapeDtypeStruct((t, d), F32),
        scratch_shapes=[pltpu.VMEM((PEER_HEADS, N_KEYS, tm), F32),
                        pltpu.VMEM((PEER_HEADS, N_KEYS, tm), F32),
                        pltpu.VMEM((PEER_HEADS, tm), F32),
                        pltpu.VMEM((d, tm), F32),
                        pltpu.VMEM((eb, tm), F32),
                        pltpu.VMEM((eb, tm), BF16)],
        compiler_params=_params(("parallel", "arbitrary")),
        name="peer",
    )(sc, h2, u_bf, vt_bf, x1, mod3, final_g)


def kernel(x, c, w_mod, b_mod, norm1_g, w_in, b_f, w_pool, pool_scale, w_out, norm2_g,
           peer_w_query, peer_sub_keys, peer_u, peer_v, final_g):
    bsz, seq, d = x.shape
    depth = w_mod.shape[0]
    t = bsz * seq
    assert d == ATTN_HEADS * HEAD_DIM and seq % INPROJ_TM == 0 and t % PEER_TM == 0
    x2d = x.reshape(t, d)
    out = x2d
    for l in range(depth):
        mod3 = _mod_call(c, w_mod[l], b_mod[l]).reshape(bsz, N_MOD, d)

        w = w_in[l]
        o_f = POOL_W + 3 * d
        w_main = jnp.concatenate([w[:, :o_f], w[:, o_f + ATTN_HEADS:]], axis=1).astype(BF16)
        w_ft = w[:, o_f:o_f + ATTN_HEADS].T.astype(BF16)
        u, q, k, v, ga, gb, ft = _inproj_call(
            x2d, mod3, norm1_g[l].reshape(1, d), w_main, w_ft,
            b_f[l].reshape(ATTN_HEADS, 1), seq)

        yb = _attn_call(q, k, v, ft.reshape(ATTN_HEADS // 2, 2, t), bsz, seq)

        n_hp = 2 * PEER_HEADS
        x1, h2, sc = _mix_call(
            u, ga, gb, yb, x2d, mod3, w_pool[l].astype(BF16), pool_scale[l].reshape(1, d),
            w_out[l].astype(BF16), norm2_g[l].reshape(1, d), peer_w_query[l].astype(BF16),
            peer_sub_keys[l].reshape(n_hp, N_KEYS, -1).astype(BF16), seq)

        assert l == depth - 1 == 0
        out = _peer_call(sc, h2, peer_u[l].astype(BF16), peer_v[l].T.astype(BF16),
                         x1, mod3, final_g.reshape(1, d), seq)
        x2d = out
    return out.reshape(bsz, seq, d)
```

```python
import functools

import jax
import jax.numpy as jnp
from jax import lax
from jax.experimental import pallas as pl
from jax.experimental.pallas import tpu as pltpu

F32 = jnp.float32
BF16 = jnp.bfloat16

EPS = 1e-6
NEG_INF = -1e30

POOL_WINDOWS = (2, 4, 8, 16)
POOL_GROUP_W = 128
POOL_W = len(POOL_WINDOWS) * POOL_GROUP_W
ATTN_HEADS = 16
HEAD_DIM = 64
PEER_HEADS = 8
N_KEYS = 128
PEER_TOPK = 16
N_MOD = 6

LANES = 128
VMEM_LIMIT = 56 * 1024 * 1024

INPROJ_TM = 512
ATTN_TQ = 512
ATTN_TK = 512
MIX_TM = 256
PEER_TM = 512
PEER_EB = 1024


def _params(sem):
    return pltpu.CompilerParams(dimension_semantics=sem, vmem_limit_bytes=VMEM_LIMIT)


def _split3(a):
    hi = a.astype(BF16)
    r1 = a - hi.astype(F32)
    mid = r1.astype(BF16)
    lo = (r1 - mid.astype(F32)).astype(BF16)
    return hi, mid, lo


def _dot(a, b):
    return jnp.dot(a, b, preferred_element_type=F32)


def _dot_nt(a, b):
    return lax.dot_general(a, b, (((1,), (1,)), ((), ())), preferred_element_type=F32)


def _rms_mod(x, g, shift, scale):
    ms = jnp.mean(x * x, axis=-1, keepdims=True)
    y = x * lax.rsqrt(ms + EPS) * g
    return y * (1.0 + scale) + shift


def _mod_kernel(c_ref, w_ref, b_ref, o_ref):
    c_hi, c_mid, _ = _split3(c_ref[...])
    w_hi, w_mid, _ = _split3(w_ref[...])
    acc = _dot(c_hi, w_hi) + _dot(c_hi, w_mid) + _dot(c_mid, w_hi)
    o_ref[...] = acc + b_ref[...]


def _mod_call(c, w_mod, b_mod):
    bsz, d = c.shape
    n = w_mod.shape[1]
    bn = 1024
    return pl.pallas_call(
        _mod_kernel,
        grid=(n // bn,),
        in_specs=[pl.BlockSpec((bsz, d), lambda j: (0, 0)),
                  pl.BlockSpec((d, bn), lambda j: (0, j)),
                  pl.BlockSpec((1, bn), lambda j: (0, j))],
        out_specs=pl.BlockSpec((bsz, bn), lambda j: (0, j)),
        out_shape=jax.ShapeDtypeStruct((bsz, n), F32),
        compiler_params=_params(("parallel",)),
        name="mod",
    )(c, w_mod, b_mod.reshape(1, n))


def _inproj_kernel(x_ref, mod_ref, g_ref, w_ref, wt_ref, wf_ref, bf_ref, sel_ref,
                   u_ref, ka_ref, ga_ref, gb_ref, qt_ref, vt_ref,
                   carry_ref, *, tiles_per_seq, d_model):
    i = pl.program_id(0)
    tm = x_ref.shape[0]
    h = _rms_mod(x_ref[...], g_ref[...], mod_ref[0, 0:1, :], mod_ref[0, 1:2, :])
    hb = h.astype(BF16)

    cw = 512
    n_c = d_model // cw

    def proj(c0):
        return _dot(hb, w_ref[:, c0:c0 + cw])

    u_ref[...] = proj(0).astype(BF16)
    base = POOL_W
    pairs_per_chunk = cw // LANES
    for c in range(n_c):
        kc = proj(base + c * cw).astype(BF16)
        for pp in range(pairs_per_chunk):
            p = c * pairs_per_chunk + pp
            ka_ref[:, p * 2 * LANES:p * 2 * LANES + LANES] = kc[:, pp * LANES:(pp + 1) * LANES]
    base += d_model
    for c in range(n_c):
        ga_ref[:, c * cw:(c + 1) * cw] = jax.nn.sigmoid(proj(base + c * cw)).astype(BF16)
    base += d_model
    for c in range(n_c):
        gb_ref[:, c * cw:(c + 1) * cw] = jax.nn.sigmoid(proj(base + c * cw)).astype(BF16)

    scale = HEAD_DIM ** -0.5
    for c in range(n_c):
        qt_ref[c * cw:(c + 1) * cw, :] = (_dot_nt(wt_ref[c * cw:(c + 1) * cw, :], hb) * scale).astype(BF16)
    for c in range(n_c):
        r0 = d_model + c * cw
        vt_ref[c * cw:(c + 1) * cw, :] = _dot_nt(wt_ref[r0:r0 + cw, :], hb).astype(BF16)

    zf = _dot(hb, wf_ref[...]) + bf_ref[...]
    logf = jnp.minimum(zf, 0.0) - jnp.log1p(jnp.exp(-jnp.abs(zf)))
    row = lax.broadcasted_iota(jnp.int32, (tm, tm), 0)
    col = lax.broadcasted_iota(jnp.int32, (tm, tm), 1)
    tril = jnp.where(row >= col, 1.0, 0.0).astype(BF16)
    hi, mid, lo = _split3(logf)
    cs = _dot(tril, hi) + _dot(tril, mid) + _dot(tril, lo)

    @pl.when(i % tiles_per_seq == 0)
    def _():
        carry_ref[...] = jnp.zeros_like(carry_ref)

    f_cum = cs + carry_ref[0:1, :]
    carry_ref[...] = jnp.broadcast_to(f_cum[tm - 1:tm, :], carry_ref.shape)
    n_hi, n_mid, n_lo = _split3(-f_cum)
    faug = _dot(n_hi, sel_ref[0]) + _dot(n_mid, sel_ref[1]) + _dot(n_lo, sel_ref[2])
    for p in range(d_model // LANES):
        ka_ref[:, p * 2 * LANES + LANES:(p + 1) * 2 * LANES] = faug[:, p * LANES:(p + 1) * LANES].astype(BF16)


def _inproj_call(x2d, mod3, norm_g, w_tok, w_feat_t, w_f, b_f, sel, seq):
    t, d = x2d.shape
    tm = INPROJ_TM
    tps = seq // tm
    row = lambda i: (i, 0)
    colb = lambda i: (0, i)
    c2 = lambda i: (0, 0)
    out_shape = (
        jax.ShapeDtypeStruct((t, POOL_W), BF16),
        jax.ShapeDtypeStruct((t, 2 * d), BF16),
        jax.ShapeDtypeStruct((t, d), BF16), jax.ShapeDtypeStruct((t, d), BF16),
        jax.ShapeDtypeStruct((d, t), BF16), jax.ShapeDtypeStruct((d, t), BF16),
    )
    out_specs = (
        pl.BlockSpec((tm, POOL_W), row),
        pl.BlockSpec((tm, 2 * d), row),
        pl.BlockSpec((tm, d), row), pl.BlockSpec((tm, d), row),
        pl.BlockSpec((d, tm), colb), pl.BlockSpec((d, tm), colb),
    )
    return pl.pallas_call(
        functools.partial(_inproj_kernel, tiles_per_seq=tps, d_model=d),
        grid=(t // tm,),
        in_specs=[pl.BlockSpec((tm, d), row),
                  pl.BlockSpec((1, N_MOD, d), lambda i: (i // tps, 0, 0)),
                  pl.BlockSpec((1, d), c2),
                  pl.BlockSpec(w_tok.shape, c2),
                  pl.BlockSpec(w_feat_t.shape, c2),
                  pl.BlockSpec(w_f.shape, c2),
                  pl.BlockSpec(b_f.shape, c2),
                  pl.BlockSpec(sel.shape, lambda i: (0, 0, 0))],
        out_specs=out_specs,
        out_shape=out_shape,
        scratch_shapes=[pltpu.VMEM((8, LANES), F32)],
        compiler_params=_params(("arbitrary",)),
        name="inproj",
    )(x2d, mod3, norm_g, w_tok, w_feat_t, w_f, b_f, sel)


def _forget_selectors():
    h = jnp.arange(LANES)[:, None]
    c = jnp.arange(ATTN_HEADS // 2 * LANES)[None, :]
    sels = []
    for r in range(3):
        hit = (h < ATTN_HEADS) & (c == (h // 2) * LANES + 3 * (h % 2) + r)
        sels.append(jnp.where(hit, 1.0, 0.0))
    return jnp.stack(sels).astype(BF16)


def _attn_kernel(qt_ref, ka_ref, vt_ref, o_ref, *, tq, tk):
    seq = ka_ref.shape[0]
    rowid = lax.broadcasted_iota(jnp.int32, (LANES, tq), 0)
    krow = lax.broadcasted_iota(jnp.int32, (tk, tq), 0)
    qcol = lax.broadcasted_iota(jnp.int32, (tk, tq), 1)
    aug = [jnp.where((rowid >= 3 * h) & (rowid < 3 * h + 3), 1.0, 0.0).astype(BF16) for h in range(2)]
    head = [jnp.where(rowid // HEAD_DIM == h, 1.0, 0.0).astype(BF16) for h in range(2)]

    def q_tile(qi, carry):
        q0 = pl.multiple_of(qi * tq, tq)
        qt = qt_ref[:, pl.ds(q0, tq)]
        ws = [jnp.concatenate([qt * head[h], aug[h]], axis=0) for h in range(2)]

        def kv_step(j, state, diag):
            k0 = pl.multiple_of(j * tk, tk)
            ka = ka_ref[pl.ds(k0, tk), :]
            new = []
            for h in range(2):
                m, l, acc = state[h]
                s = _dot(ka, ws[h])
                if diag is not None:
                    s = jnp.where(krow + diag * tk <= qcol, s, NEG_INF)
                m_new = jnp.maximum(m, jnp.max(s, axis=0, keepdims=True))
                alpha = jnp.exp(m - m_new)
                p = jnp.exp(s - m_new)
                l = alpha * l + jnp.sum(p, axis=0, keepdims=True)
                vt = vt_ref[h * HEAD_DIM:(h + 1) * HEAD_DIM, pl.ds(k0, tk)]
                acc = alpha * acc + _dot(vt, p.astype(BF16))
                new.append((m_new, l, acc))
            return tuple(new)

        init = tuple((jnp.full((1, tq), NEG_INF, F32), jnp.zeros((1, tq), F32),
                      jnp.zeros((HEAD_DIM, tq), F32)) for _ in range(2))
        n_full = qi * (tq // tk)
        state = lax.fori_loop(0, n_full, lambda j, st: kv_step(j, st, None), init)
        for dblk in range(tq // tk):
            state = kv_step(n_full + dblk, state, dblk)
        out_t = jnp.concatenate([state[0][2] / state[0][1], state[1][2] / state[1][1]], axis=0)
        o_ref[pl.ds(q0, tq), :] = out_t.T.astype(o_ref.dtype)
        return carry

    lax.fori_loop(0, seq // tq, q_tile, 0)


def _attn_call(qt, ka, vt, bsz, seq):
    d, t = qt.shape
    n_pairs = d // LANES
    return pl.pallas_call(
        functools.partial(_attn_kernel, tq=ATTN_TQ, tk=ATTN_TK),
        grid=(bsz, n_pairs),
        in_specs=[pl.BlockSpec((LANES, seq), lambda b, p: (p, b)),
                  pl.BlockSpec((seq, 2 * LANES), lambda b, p: (b, p)),
                  pl.BlockSpec((LANES, seq), lambda b, p: (p, b))],
        out_specs=pl.BlockSpec((seq, LANES), lambda b, p: (b, p)),
        out_shape=jax.ShapeDtypeStruct((t, d), BF16),
        compiler_params=_params(("parallel", "parallel")),
        name="attn",
    )(qt, ka, vt)


def _mix_kernel(uc_ref, up_ref, ga_ref, gb_ref, yb_ref, x_ref, mod_ref, wp_ref, ps_ref,
                wo_ref, g2_ref, wq_ref, sk_ref,
                x1_ref, h2_ref, sc_ref, y_ref, *, tiles_per_seq):
    i = pl.program_id(0)
    tm = x_ref.shape[0]
    halo = LANES
    first = (i % tiles_per_seq) == 0
    pos0 = (i % tiles_per_seq) * tm
    r_d = lax.broadcasted_iota(jnp.int32, (tm, tm), 0)
    c_d = lax.broadcasted_iota(jnp.int32, (tm, tm), 1)
    r_o = lax.broadcasted_iota(jnp.int32, (tm, halo), 0)
    c_o = lax.broadcasted_iota(jnp.int32, (tm, halo), 1)
    pos = pos0 + lax.broadcasted_iota(jnp.int32, (tm, 1), 0)
    gw = wp_ref.shape[2]
    for g, w in enumerate(POOL_WINDOWS):
        lo, hi = g * POOL_GROUP_W, (g + 1) * POOL_GROUP_W
        u_cur = uc_ref[:, lo:hi]
        u_prev = up_ref[tm - halo:tm, lo:hi]
        lag = r_d - c_d
        band_d = jnp.where((lag >= 0) & (lag < w), 1.0, 0.0).astype(BF16)
        band_o = jnp.where((r_o + halo - c_o < w) & jnp.logical_not(first), 1.0, 0.0).astype(BF16)
        wsum = _dot(band_d, u_cur) + _dot(band_o, u_prev)
        cnt = jnp.minimum(pos + 1, w).astype(F32)
        pooled = wsum / cnt - u_cur.astype(F32)
        ya = _dot(pooled.astype(BF16), wp_ref[g]) * ps_ref[:, g * gw:(g + 1) * gw]
        sl = slice(g * gw, (g + 1) * gw)
        y = ga_ref[:, sl].astype(F32) * ya + gb_ref[:, sl].astype(F32) * yb_ref[:, sl].astype(F32)
        y_ref[:, sl] = y.astype(BF16)

    o = _dot(y_ref[...], wo_ref[...])
    x1 = x_ref[...] + mod_ref[0, 2:3, :] * o
    x1_ref[...] = x1
    h2 = _rms_mod(x1, g2_ref[...], mod_ref[0, 3:4, :], mod_ref[0, 4:5, :]).astype(BF16)
    h2_ref[...] = h2
    n_hp = sk_ref.shape[0]
    half = sk_ref.shape[2]
    for hp in range(n_hp):
        qp = _dot(h2, wq_ref[:, hp * half:(hp + 1) * half]).astype(BF16)
        sc_ref[hp] = _dot_nt(sk_ref[hp], qp)


def _mix_call(u, ga, gb, yb, x2d, mod3, w_pool, pool_scale, w_out, norm2_g, w_query, sub_keys, seq):
    t, d = x2d.shape
    tm = MIX_TM
    tps = seq // tm
    row = lambda i: (i, 0)
    c2 = lambda i: (0, 0)
    c3 = lambda i: (0, 0, 0)
    n_hp = sub_keys.shape[0]
    return pl.pallas_call(
        functools.partial(_mix_kernel, tiles_per_seq=tps),
        grid=(t // tm,),
        in_specs=[pl.BlockSpec((tm, POOL_W), row),
                  pl.BlockSpec((tm, POOL_W), lambda i: (jnp.maximum(i - 1, 0), 0)),
                  pl.BlockSpec((tm, d), row), pl.BlockSpec((tm, d), row), pl.BlockSpec((tm, d), row),
                  pl.BlockSpec((tm, d), row),
                  pl.BlockSpec((1, N_MOD, d), lambda i: (i // tps, 0, 0)),
                  pl.BlockSpec(w_pool.shape, c3),
                  pl.BlockSpec((1, d), c2),
                  pl.BlockSpec((d, d), c2),
                  pl.BlockSpec((1, d), c2),
                  pl.BlockSpec(w_query.shape, c2),
                  pl.BlockSpec(sub_keys.shape, c3)],
        out_specs=(pl.BlockSpec((tm, d), row), pl.BlockSpec((tm, d), row),
                   pl.BlockSpec((n_hp, N_KEYS, tm), lambda i: (0, 0, i))),
        out_shape=(jax.ShapeDtypeStruct((t, d), F32), jax.ShapeDtypeStruct((t, d), BF16),
                   jax.ShapeDtypeStruct((n_hp, N_KEYS, t), F32)),
        scratch_shapes=[pltpu.VMEM((tm, d), BF16)],
        compiler_params=_params(("parallel",)),
        name="mix",
    )(u, u, ga, gb, yb, x2d, mod3, w_pool, pool_scale, w_out, norm2_g, w_query, sub_keys)


def _top16_rows(s):
    cur = s
    rows = []
    for _ in range(PEER_TOPK):
        m = jnp.max(cur, axis=0, keepdims=True)
        rows.append(m)
        cur = jnp.where(cur == m, -jnp.inf, cur)
    return rows


def _stack_rows(rows, tm):
    n = len(rows)
    rid = lax.broadcasted_iota(jnp.int32, (n, tm), 0)
    arr = jnp.zeros((n, tm), F32)
    for r, v in enumerate(rows):
        arr = jnp.where(rid == r, v, arr)
    return arr


def _candidates(rows1, arr1_hi, rows2, arr2, combine):
    pieces = [combine(rows1[0], arr2)]
    for a in range(1, 8):
        pieces.append(combine(rows1[a], arr2[0:8]))
    pieces.append(combine(arr1_hi, rows2[0]))
    return jnp.concatenate(pieces, axis=0)


def _peer_stats(sc_ref, e1_ref, e2_ref, pthr_ref, h):
    tm = sc_ref.shape[2]
    s1 = sc_ref[2 * h]
    s2 = sc_ref[2 * h + 1]
    t1 = _top16_rows(s1)
    t2 = _top16_rows(s2)
    t1_hi = _stack_rows(t1[8:], tm)
    t2_arr = _stack_rows(t2, tm)
    cand = _candidates(t1, t1_hi, t2, t2_arr, lambda a, b: a + b)
    n_c = cand.shape[0]
    cid = lax.broadcasted_iota(jnp.int32, (n_c, tm), 0)
    cur = cand
    z = jnp.zeros((1, tm), F32)
    m0 = None
    for r in range(PEER_TOPK):
        m = jnp.max(cur, axis=0, keepdims=True)
        if r == 0:
            m0 = m
        first = jnp.min(jnp.where(cur == m, cid, n_c), axis=0, keepdims=True)
        cur = jnp.where(cid == first, -jnp.inf, cur)
        z = z + jnp.exp(m - m0)
    selected = cur == -jnp.inf
    inv_z = 1.0 / z
    e1_rows = [jnp.exp(v - t1[0]) * inv_z for v in t1]
    e2_rows = [jnp.exp(v - t2[0]) for v in t2]
    e1_hi = _stack_rows(e1_rows[8:], tm)
    e2_arr = _stack_rows(e2_rows, tm)
    p_cand = _candidates(e1_rows, e1_hi, e2_rows, e2_arr, lambda a, b: a * b)
    pthr = jnp.min(jnp.where(selected, p_cand, jnp.inf), axis=0, keepdims=True)
    e1_ref[h] = jnp.exp(s1 - t1[0]) * inv_z
    e2_ref[h] = jnp.exp(s2 - t2[0])
    pthr_ref[pl.ds(h, 1), :] = pthr


def _peer_kernel(sc_ref, h2_ref, u_ref, vt_ref, x1_ref, mod_ref, fg_ref, o_ref,
                 e1_ref, e2_ref, pthr_ref, acc_ref, a_ref, y_ref):
    j = pl.program_id(1)
    nj = pl.num_programs(1)
    n_heads = e1_ref.shape[0]
    rows_per_blk = u_ref.shape[0] // N_KEYS

    @pl.when(j == 0)
    def _():
        def head(h, carry):
            _peer_stats(sc_ref, e1_ref, e2_ref, pthr_ref, h)
            return carry
        lax.fori_loop(0, n_heads, head, 0)
        acc_ref[...] = jnp.zeros_like(acc_ref)

    a_ref[...] = _dot_nt(u_ref[...], h2_ref[...])

    def key_row(ii, carry):
        i_key = j * rows_per_blk + ii
        g = jnp.zeros((N_KEYS, a_ref.shape[1]), F32)
        for h in range(n_heads):
            p = e1_ref[h, pl.ds(i_key, 1), :] * e2_ref[h]
            g = g + jnp.where(p >= pthr_ref[h:h + 1, :], p, 0.0)
        r0 = pl.multiple_of(ii * N_KEYS, N_KEYS)
        a = a_ref[pl.ds(r0, N_KEYS), :]
        act = 0.5 * a * (1.0 + lax.erf(a * (0.5 ** 0.5)))
        y_ref[pl.ds(r0, N_KEYS), :] = (act * g).astype(BF16)
        return carry

    lax.fori_loop(0, rows_per_blk, key_row, 0)
    acc_ref[...] += _dot(vt_ref[...], y_ref[...])

    @pl.when(j == nj - 1)
    def _():
        peer = acc_ref[...].T
        x2 = x1_ref[...] + mod_ref[0, 5:6, :] * peer
        ms = jnp.mean(x2 * x2, axis=-1, keepdims=True)
        o_ref[...] = x2 * lax.rsqrt(ms + EPS) * fg_ref[...]


def _peer_call(sc, h2, u_bf, vt_bf, x1, mod3, final_g, seq):
    t, d = x1.shape
    tm, eb = PEER_TM, PEER_EB
    tps = seq // tm
    n_hp = sc.shape[0]
    n_exp = u_bf.shape[0]
    row = lambda i, j: (i, 0)
    return pl.pallas_call(
        _peer_kernel,
        grid=(t // tm, n_exp // eb),
        in_specs=[pl.BlockSpec((n_hp, N_KEYS, tm), lambda i, j: (0, 0, i)),
                  pl.BlockSpec((tm, d), row),
                  pl.BlockSpec((eb, d), lambda i, j: (j, 0)),
                  pl.BlockSpec((d, eb), lambda i, j: (0, j)),
                  pl.BlockSpec((tm, d), row),
                  pl.BlockSpec((1, N_MOD, d), lambda i, j: (i // tps, 0, 0)),
                  pl.BlockSpec((1, d), lambda i, j: (0, 0))],
        out_specs=pl.BlockSpec((tm, d), row),
        out_shape=jax.ShapeDtypeStruct((t, d), F32),
        scratch_shapes=[pltpu.VMEM((PEER_HEADS, N_KEYS, tm), F32),
                        pltpu.VMEM((PEER_HEADS, N_KEYS, tm), F32),
                        pltpu.VMEM((PEER_HEADS, tm), F32),
                        pltpu.VMEM((d, tm), F32),
                        pltpu.VMEM((eb, tm), F32),
                        pltpu.VMEM((eb, tm), BF16)],
        compiler_params=_params(("parallel", "arbitrary")),
        name="peer",
    )(sc, h2, u_bf, vt_bf, x1, mod3, final_g)


def kernel(x, c, w_mod, b_mod, norm1_g, w_in, b_f, w_pool, pool_scale, w_out, norm2_g,
           peer_w_query, peer_sub_keys, peer_u, peer_v, final_g):
    bsz, seq, d = x.shape
    depth = w_mod.shape[0]
    t = bsz * seq
    assert d == ATTN_HEADS * HEAD_DIM and seq % INPROJ_TM == 0 and t % PEER_TM == 0
    x2d = x.reshape(t, d)
    out = x2d
    for l in range(depth):
        mod3 = _mod_call(c, w_mod[l], b_mod[l]).reshape(bsz, N_MOD, d)

        w = w_in[l]
        o_q, o_k, o_v = POOL_W, POOL_W + d, POOL_W + 2 * d
        o_f = POOL_W + 3 * d
        o_g = o_f + ATTN_HEADS
        w_tok = jnp.concatenate([w[:, :o_q], w[:, o_k:o_v], w[:, o_g:]], axis=1).astype(BF16)
        w_feat_t = jnp.concatenate([w[:, o_q:o_k], w[:, o_v:o_f]], axis=1).T.astype(BF16)
        w_f = jnp.pad(w[:, o_f:o_g], ((0, 0), (0, LANES - ATTN_HEADS))).astype(BF16)
        b_f_row = jnp.pad(b_f[l], (0, LANES - ATTN_HEADS)).reshape(1, LANES)
        u, ka, ga, gb, qt, vt = _inproj_call(
            x2d, mod3, norm1_g[l].reshape(1, d), w_tok, w_feat_t, w_f, b_f_row,
            _forget_selectors(), seq)

        yb = _attn_call(qt, ka, vt, bsz, seq)

        n_hp = 2 * PEER_HEADS
        x1, h2, sc = _mix_call(
            u, ga, gb, yb, x2d, mod3, w_pool[l].astype(BF16), pool_scale[l].reshape(1, d),
            w_out[l].astype(BF16), norm2_g[l].reshape(1, d), peer_w_query[l].astype(BF16),
            peer_sub_keys[l].reshape(n_hp, N_KEYS, -1).astype(BF16), seq)

        assert l == depth - 1 == 0
        out = _peer_call(sc, h2, peer_u[l].astype(BF16), peer_v[l].T.astype(BF16),
                         x1, mod3, final_g.reshape(1, d), seq)
        x2d = out
    return out.reshape(bsz, seq, d)
```

```python
import functools

import jax
import jax.numpy as jnp
from jax import lax
from jax.experimental import pallas as pl
from jax.experimental.pallas import tpu as pltpu

F32 = jnp.float32
BF16 = jnp.bfloat16

EPS = 1e-6
NEG_INF = -1e30

POOL_WINDOWS = (2, 4, 8, 16)
POOL_GROUP_W = 128
POOL_W = len(POOL_WINDOWS) * POOL_GROUP_W
ATTN_HEADS = 16
HEAD_DIM = 64
PEER_HEADS = 8
N_KEYS = 128
PEER_TOPK = 16
N_MOD = 6

LANES = 128
VMEM_LIMIT = 56 * 1024 * 1024

INPROJ_TM = 512
ATTN_TQ = 512
ATTN_TK = 512
MIX_TM = 256
PEER_TM = 512
PEER_EB = 1024


def _params(sem):
    return pltpu.CompilerParams(dimension_semantics=sem, vmem_limit_bytes=VMEM_LIMIT)


def _split3(a):
    hi = a.astype(BF16)
    r1 = a - hi.astype(F32)
    mid = r1.astype(BF16)
    lo = (r1 - mid.astype(F32)).astype(BF16)
    return hi, mid, lo


def _dot(a, b):
    return jnp.dot(a, b, preferred_element_type=F32)


def _dot_nt(a, b):
    return lax.dot_general(a, b, (((1,), (1,)), ((), ())), preferred_element_type=F32)


def _rms_mod(x, g, shift, scale):
    ms = jnp.mean(x * x, axis=-1, keepdims=True)
    y = x * lax.rsqrt(ms + EPS) * g
    return y * (1.0 + scale) + shift


def _mod_kernel(c_ref, w_ref, b_ref, o_ref):
    c_hi, c_mid, _ = _split3(c_ref[...])
    w_hi, w_mid, _ = _split3(w_ref[...])
    acc = _dot(c_hi, w_hi) + _dot(c_hi, w_mid) + _dot(c_mid, w_hi)
    o_ref[...] = acc + b_ref[...]


def _mod_call(c, w_mod, b_mod):
    bsz, d = c.shape
    n = w_mod.shape[1]
    bn = 1024
    return pl.pallas_call(
        _mod_kernel,
        grid=(n // bn,),
        in_specs=[pl.BlockSpec((bsz, d), lambda j: (0, 0)),
                  pl.BlockSpec((d, bn), lambda j: (0, j)),
                  pl.BlockSpec((1, bn), lambda j: (0, j))],
        out_specs=pl.BlockSpec((bsz, bn), lambda j: (0, j)),
        out_shape=jax.ShapeDtypeStruct((bsz, n), F32),
        compiler_params=_params(("parallel",)),
        name="mod",
    )(c, w_mod, b_mod.reshape(1, n))


def _inproj_kernel(x_ref, mod_ref, g_ref, w_ref, wt_ref, wf_ref, bf_ref, sel_ref,
                   u_ref, ka_ref, ga_ref, gb_ref, qt_ref, vt_ref,
                   carry_ref, *, tiles_per_seq, d_model):
    i = pl.program_id(0)
    tm = x_ref.shape[0]
    h = _rms_mod(x_ref[...], g_ref[...], mod_ref[0, 0:1, :], mod_ref[0, 1:2, :])
    hb = h.astype(BF16)

    cw = 512
    n_c = d_model // cw

    def proj(c0):
        return _dot(hb, w_ref[:, c0:c0 + cw])

    u_ref[...] = proj(0).astype(BF16)
    base = POOL_W
    pairs_per_chunk = cw // LANES
    for c in range(n_c):
        kc = proj(base + c * cw).astype(BF16)
        for pp in range(pairs_per_chunk):
            p = c * pairs_per_chunk + pp
            ka_ref[:, p * 2 * LANES:p * 2 * LANES + LANES] = kc[:, pp * LANES:(pp + 1) * LANES]
    base += d_model
    for c in range(n_c):
        ga_ref[:, c * cw:(c + 1) * cw] = jax.nn.sigmoid(proj(base + c * cw)).astype(BF16)
    base += d_model
    for c in range(n_c):
        gb_ref[:, c * cw:(c + 1) * cw] = jax.nn.sigmoid(proj(base + c * cw)).astype(BF16)

    scale = HEAD_DIM ** -0.5
    for c in range(n_c):
        qt_ref[c * cw:(c + 1) * cw, :] = (_dot_nt(wt_ref[c * cw:(c + 1) * cw, :], hb) * scale).astype(BF16)
    for c in range(n_c):
        r0 = d_model + c * cw
        vt_ref[c * cw:(c + 1) * cw, :] = _dot_nt(wt_ref[r0:r0 + cw, :], hb).astype(BF16)

    zf = _dot(hb, wf_ref[...]) + bf_ref[...]
    logf = jnp.minimum(zf, 0.0) - jnp.log1p(jnp.exp(-jnp.abs(zf)))
    row = lax.broadcasted_iota(jnp.int32, (tm, tm), 0)
    col = lax.broadcasted_iota(jnp.int32, (tm, tm), 1)
    tril = jnp.where(row >= col, 1.0, 0.0).astype(BF16)
    hi, mid, lo = _split3(logf)
    cs = _dot(tril, hi) + _dot(tril, mid) + _dot(tril, lo)

    @pl.when(i % tiles_per_seq == 0)
    def _():
        carry_ref[...] = jnp.zeros_like(carry_ref)

    f_cum = cs + carry_ref[0:1, :]
    carry_ref[...] = jnp.broadcast_to(f_cum[tm - 1:tm, :], carry_ref.shape)
    n_hi, n_mid, n_lo = _split3(-f_cum)
    faug = _dot(n_hi, sel_ref[0]) + _dot(n_mid, sel_ref[1]) + _dot(n_lo, sel_ref[2])
    for p in range(d_model // LANES):
        ka_ref[:, p * 2 * LANES + LANES:(p + 1) * 2 * LANES] = faug[:, p * LANES:(p + 1) * LANES].astype(BF16)


def _inproj_call(x2d, mod3, norm_g, w_tok, w_feat_t, w_f, b_f, sel, seq):
    t, d = x2d.shape
    tm = INPROJ_TM
    tps = seq // tm
    row = lambda i: (i, 0)
    colb = lambda i: (0, i)
    c2 = lambda i: (0, 0)
    out_shape = (
        jax.ShapeDtypeStruct((t, POOL_W), BF16),
        jax.ShapeDtypeStruct((t, 2 * d), BF16),
        jax.ShapeDtypeStruct((t, d), BF16), jax.ShapeDtypeStruct((t, d), BF16),
        jax.ShapeDtypeStruct((d, t), BF16), jax.ShapeDtypeStruct((d, t), BF16),
    )
    out_specs = (
        pl.BlockSpec((tm, POOL_W), row),
        pl.BlockSpec((tm, 2 * d), row),
        pl.BlockSpec((tm, d), row), pl.BlockSpec((tm, d), row),
        pl.BlockSpec((d, tm), colb), pl.BlockSpec((d, tm), colb),
    )
    return pl.pallas_call(
        functools.partial(_inproj_kernel, tiles_per_seq=tps, d_model=d),
        grid=(t // tm,),
        in_specs=[pl.BlockSpec((tm, d), row),
                  pl.BlockSpec((1, N_MOD, d), lambda i: (i // tps, 0, 0)),
                  pl.BlockSpec((1, d), c2),
                  pl.BlockSpec(w_tok.shape, c2),
                  pl.BlockSpec(w_feat_t.shape, c2),
                  pl.BlockSpec(w_f.shape, c2),
                  pl.BlockSpec(b_f.shape, c2),
                  pl.BlockSpec(sel.shape, lambda i: (0, 0, 0))],
        out_specs=out_specs,
        out_shape=out_shape,
        scratch_shapes=[pltpu.VMEM((8, LANES), F32)],
        compiler_params=_params(("arbitrary",)),
        name="inproj",
    )(x2d, mod3, norm_g, w_tok, w_feat_t, w_f, b_f, sel)


def _forget_selectors():
    h = jnp.arange(LANES)[:, None]
    c = jnp.arange(ATTN_HEADS // 2 * LANES)[None, :]
    sels = []
    for r in range(3):
        hit = (h < ATTN_HEADS) & (c == (h // 2) * LANES + 3 * (h % 2) + r)
        sels.append(jnp.where(hit, 1.0, 0.0))
    return jnp.stack(sels).astype(BF16)


def _attn_kernel(qt_ref, ka_ref, vt_ref, o_ref, *, tq, tk):
    seq = ka_ref.shape[0]
    rowid = lax.broadcasted_iota(jnp.int32, (LANES, tq), 0)
    krow = lax.broadcasted_iota(jnp.int32, (tk, tq), 0)
    qcol = lax.broadcasted_iota(jnp.int32, (tk, tq), 1)
    aug = [jnp.where((rowid >= 3 * h) & (rowid < 3 * h + 3), 1.0, 0.0).astype(BF16) for h in range(2)]
    head = [jnp.where(rowid // HEAD_DIM == h, 1.0, 0.0).astype(BF16) for h in range(2)]

    def q_tile(qi, carry):
        q0 = pl.multiple_of(qi * tq, tq)
        qt = qt_ref[:, pl.ds(q0, tq)]
        ws = [jnp.concatenate([qt * head[h], aug[h]], axis=0) for h in range(2)]

        def kv_step(j, state, diag):
            k0 = pl.multiple_of(j * tk, tk)
            ka = ka_ref[pl.ds(k0, tk), :]
            new = []
            for h in range(2):
                m, l, acc = state[h]
                s = _dot(ka, ws[h])
                if diag is not None:
                    s = jnp.where(krow + diag * tk <= qcol, s, NEG_INF)
                m_new = jnp.maximum(m, jnp.max(s, axis=0, keepdims=True))
                alpha = jnp.exp(m - m_new)
                p = jnp.exp(s - m_new)
                l = alpha * l + jnp.sum(p, axis=0, keepdims=True)
                vt = vt_ref[h * HEAD_DIM:(h + 1) * HEAD_DIM, pl.ds(k0, tk)]
                acc = alpha * acc + _dot(vt, p.astype(BF16))
                new.append((m_new, l, acc))
            return tuple(new)

        init = tuple((jnp.full((1, tq), NEG_INF, F32), jnp.zeros((1, tq), F32),
                      jnp.zeros((HEAD_DIM, tq), F32)) for _ in range(2))
        n_full = qi * (tq // tk)
        state = lax.fori_loop(0, n_full, lambda j, st: kv_step(j, st, None), init)
        for dblk in range(tq // tk):
            state = kv_step(n_full + dblk, state, dblk)
        out_t = jnp.concatenate([state[0][2] / state[0][1], state[1][2] / state[1][1]], axis=0)
        o_ref[pl.ds(q0, tq), :] = out_t.T.astype(o_ref.dtype)
        return carry

    lax.fori_loop(0, seq // tq, q_tile, 0)


def _attn_call(qt, ka, vt, bsz, seq):
    d, t = qt.shape
    n_pairs = d // LANES
    return pl.pallas_call(
        functools.partial(_attn_kernel, tq=ATTN_TQ, tk=ATTN_TK),
        grid=(bsz, n_pairs),
        in_specs=[pl.BlockSpec((LANES, seq), lambda b, p: (p, b)),
                  pl.BlockSpec((seq, 2 * LANES), lambda b, p: (b, p)),
                  pl.BlockSpec((LANES, seq), lambda b, p: (p, b))],
        out_specs=pl.BlockSpec((seq, LANES), lambda b, p: (b, p)),
        out_shape=jax.ShapeDtypeStruct((t, d), BF16),
        compiler_params=_params(("parallel", "parallel")),
        name="attn",
    )(qt, ka, vt)


def _mix_kernel(uc_ref, up_ref, ga_ref, gb_ref, yb_ref, x_ref, mod_ref, wp_ref, ps_ref,
                wo_ref, g2_ref, wq_ref, sk_ref,
                x1_ref, h2_ref, sc_ref, y_ref, *, tiles_per_seq):
    i = pl.program_id(0)
    tm = x_ref.shape[0]
    halo = LANES
    first = (i % tiles_per_seq) == 0
    pos0 = (i % tiles_per_seq) * tm
    r_d = lax.broadcasted_iota(jnp.int32, (tm, tm), 0)
    c_d = lax.broadcasted_iota(jnp.int32, (tm, tm), 1)
    r_o = lax.broadcasted_iota(jnp.int32, (tm, halo), 0)
    c_o = lax.broadcasted_iota(jnp.int32, (tm, halo), 1)
    pos = pos0 + lax.broadcasted_iota(jnp.int32, (tm, 1), 0)
    gw = wp_ref.shape[2]
    for g, w in enumerate(POOL_WINDOWS):
        lo, hi = g * POOL_GROUP_W, (g + 1) * POOL_GROUP_W
        u_cur = uc_ref[:, lo:hi]
        u_prev = up_ref[tm - halo:tm, lo:hi]
        lag = r_d - c_d
        band_d = jnp.where((lag >= 0) & (lag < w), 1.0, 0.0).astype(BF16)
        band_o = jnp.where((r_o + halo - c_o < w) & jnp.logical_not(first), 1.0, 0.0).astype(BF16)
        wsum = _dot(band_d, u_cur) + _dot(band_o, u_prev)
        cnt = jnp.minimum(pos + 1, w).astype(F32)
        pooled = wsum / cnt - u_cur.astype(F32)
        ya = _dot(pooled.astype(BF16), wp_ref[g]) * ps_ref[:, g * gw:(g + 1) * gw]
        sl = slice(g * gw, (g + 1) * gw)
        y = ga_ref[:, sl].astype(F32) * ya + gb_ref[:, sl].astype(F32) * yb_ref[:, sl].astype(F32)
        y_ref[:, sl] = y.astype(BF16)

    o = _dot(y_ref[...], wo_ref[...])
    x1 = x_ref[...] + mod_ref[0, 2:3, :] * o
    x1_ref[...] = x1
    h2 = _rms_mod(x1, g2_ref[...], mod_ref[0, 3:4, :], mod_ref[0, 4:5, :]).astype(BF16)
    h2_ref[...] = h2
    n_hp = sk_ref.shape[0]
    half = sk_ref.shape[2]
    for hp in range(n_hp):
        qp = _dot(h2, wq_ref[:, hp * half:(hp + 1) * half]).astype(BF16)
        sc_ref[hp] = _dot_nt(sk_ref[hp], qp)


def _mix_call(u, ga, gb, yb, x2d, mod3, w_pool, pool_scale, w_out, norm2_g, w_query, sub_keys, seq):
    t, d = x2d.shape
    tm = MIX_TM
    tps = seq // tm
    row = lambda i: (i, 0)
    c2 = lambda i: (0, 0)
    c3 = lambda i: (0, 0, 0)
    n_hp = sub_keys.shape[0]
    return pl.pallas_call(
        functools.partial(_mix_kernel, tiles_per_seq=tps),
        grid=(t // tm,),
        in_specs=[pl.BlockSpec((tm, POOL_W), row),
                  pl.BlockSpec((tm, POOL_W), lambda i: (jnp.maximum(i - 1, 0), 0)),
                  pl.BlockSpec((tm, d), row), pl.BlockSpec((tm, d), row), pl.BlockSpec((tm, d), row),
                  pl.BlockSpec((tm, d), row),
                  pl.BlockSpec((1, N_MOD, d), lambda i: (i // tps, 0, 0)),
                  pl.BlockSpec(w_pool.shape, c3),
                  pl.BlockSpec((1, d), c2),
                  pl.BlockSpec((d, d), c2),
                  pl.BlockSpec((1, d), c2),
                  pl.BlockSpec(w_query.shape, c2),
                  pl.BlockSpec(sub_keys.shape, c3)],
        out_specs=(pl.BlockSpec((tm, d), row), pl.BlockSpec((tm, d), row),
                   pl.BlockSpec((n_hp, N_KEYS, tm), lambda i: (0, 0, i))),
        out_shape=(jax.ShapeDtypeStruct((t, d), F32), jax.ShapeDtypeStruct((t, d), BF16),
                   jax.ShapeDtypeStruct((n_hp, N_KEYS, t), F32)),
        scratch_shapes=[pltpu.VMEM((tm, d), BF16)],
        compiler_params=_params(("parallel",)),
        name="mix",
    )(u, u, ga, gb, yb, x2d, mod3, w_pool, pool_scale, w_out, norm2_g, w_query, sub_keys)


def _top16_rows(s):
    cur = s
    rows = []
    for _ in range(PEER_TOPK):
        m = jnp.max(cur, axis=0, keepdims=True)
        rows.append(m)
        cur = jnp.where(cur == m, -jnp.inf, cur)
    return rows


def _stack_rows(rows, tm):
    n = len(rows)
    rid = lax.broadcasted_iota(jnp.int32, (n, tm), 0)
    arr = jnp.zeros((n, tm), F32)
    for r, v in enumerate(rows):
        arr = jnp.where(rid == r, v, arr)
    return arr


def _candidates(rows1, arr1_hi, rows2, arr2, combine):
    pieces = [combine(rows1[0], arr2)]
    for a in range(1, 8):
        pieces.append(combine(rows1[a], arr2[0:8]))
    pieces.append(combine(arr1_hi, rows2[0]))
    return jnp.concatenate(pieces, axis=0)


def _peer_stats(sc_ref, e1_ref, e2_ref, pthr_ref, h):
    tm = sc_ref.shape[2]
    s1 = sc_ref[2 * h]
    s2 = sc_ref[2 * h + 1]
    t1 = _top16_rows(s1)
    t2 = _top16_rows(s2)
    t1_hi = _stack_rows(t1[8:], tm)
    t2_arr = _stack_rows(t2, tm)
    cand = _candidates(t1, t1_hi, t2, t2_arr, lambda a, b: a + b)
    n_c = cand.shape[0]
    cid = lax.broadcasted_iota(jnp.int32, (n_c, tm), 0)
    cur = cand
    z = jnp.zeros((1, tm), F32)
    m0 = None
    for r in range(PEER_TOPK):
        m = jnp.max(cur, axis=0, keepdims=True)
        if r == 0:
            m0 = m
        first = jnp.min(jnp.where(cur == m, cid, n_c), axis=0, keepdims=True)
        cur = jnp.where(cid == first, -jnp.inf, cur)
        z = z + jnp.exp(m - m0)
    selected = cur == -jnp.inf
    inv_z = 1.0 / z
    e1_rows = [jnp.exp(v - t1[0]) * inv_z for v in t1]
    e2_rows = [jnp.exp(v - t2[0]) for v in t2]
    e1_hi = _stack_rows(e1_rows[8:], tm)
    e2_arr = _stack_rows(e2_rows, tm)
    p_cand = _candidates(e1_rows, e1_hi, e2_rows, e2_arr, lambda a, b: a * b)
    pthr = jnp.min(jnp.where(selected, p_cand, jnp.inf), axis=0, keepdims=True)
    e1_ref[h] = jnp.exp(s1 - t1[0]) * inv_z
    e2_ref[h] = jnp.exp(s2 - t2[0])
    pthr_ref[pl.ds(h, 1), :] = pthr


def _peer_kernel(sc_ref, h2_ref, u_ref, vt_ref, x1_ref, mod_ref, fg_ref, o_ref,
                 e1_ref, e2_ref, pthr_ref, acc_ref, a_ref, y_ref):
    i = pl.program_id(0)
    s = pl.program_id(1)
    n_blk = pl.num_programs(1) - 1
    n_heads = e1_ref.shape[0]
    eb = u_ref.shape[0]
    rows_per_blk = eb // N_KEYS
    kc = 2 * N_KEYS

    @pl.when((i == 0) & (s == 0))
    def _():
        a_ref[...] = jnp.zeros_like(a_ref)
        acc_ref[...] = jnp.zeros_like(acc_ref)

    @pl.when(s == 0)
    def _():
        def head(h, carry):
            _peer_stats(sc_ref, e1_ref, e2_ref, pthr_ref, h)
            return carry
        lax.fori_loop(0, n_heads, head, 0)

    def step(w, r):
        a_ref[w] = _dot_nt(u_ref[...], h2_ref[...])
        blk = jnp.maximum(s - 1, 0)
        for ii in range(rows_per_blk):
            i_key = blk * rows_per_blk + ii
            g = jnp.zeros((N_KEYS, a_ref.shape[2]), F32)
            for h in range(n_heads):
                p = e1_ref[h, pl.ds(i_key, 1), :] * e2_ref[h]
                g = g + jnp.where(p >= pthr_ref[h:h + 1, :], p, 0.0)
            a = a_ref[r, ii * N_KEYS:(ii + 1) * N_KEYS, :]
            act = 0.5 * a * (1.0 + lax.erf(a * (0.5 ** 0.5)))
            y_ref[ii * N_KEYS:(ii + 1) * N_KEYS, :] = (act * g).astype(BF16)
            if (ii + 1) * N_KEYS % kc == 0:
                c0 = (ii + 1) * N_KEYS - kc
                acc_ref[...] += _dot(vt_ref[:, c0:c0 + kc], y_ref[c0:c0 + kc, :])

    @pl.when(s % 2 == 0)
    def _():
        step(0, 1)

    @pl.when(s % 2 == 1)
    def _():
        step(1, 0)

    @pl.when(s == 0)
    def _():
        acc_ref[...] = jnp.zeros_like(acc_ref)

    @pl.when(s == n_blk)
    def _():
        peer = acc_ref[...].T
        x2 = x1_ref[...] + mod_ref[0, 5:6, :] * peer
        ms = jnp.mean(x2 * x2, axis=-1, keepdims=True)
        o_ref[...] = x2 * lax.rsqrt(ms + EPS) * fg_ref[...]


def _peer_call(sc, h2, u_bf, vt_bf, x1, mod3, final_g, seq):
    t, d = x1.shape
    tm, eb = PEER_TM, PEER_EB
    tps = seq // tm
    n_hp = sc.shape[0]
    n_exp = u_bf.shape[0]
    row = lambda i, j: (i, 0)
    n_blk = n_exp // eb
    return pl.pallas_call(
        _peer_kernel,
        grid=(t // tm, n_blk + 1),
        in_specs=[pl.BlockSpec((n_hp, N_KEYS, tm), lambda i, j: (0, 0, i)),
                  pl.BlockSpec((tm, d), row),
                  pl.BlockSpec((eb, d), lambda i, j: (jnp.minimum(j, n_blk - 1), 0)),
                  pl.BlockSpec((d, eb), lambda i, j: (0, jnp.maximum(j - 1, 0))),
                  pl.BlockSpec((tm, d), row),
                  pl.BlockSpec((1, N_MOD, d), lambda i, j: (i // tps, 0, 0)),
                  pl.BlockSpec((1, d), lambda i, j: (0, 0))],
        out_specs=pl.BlockSpec((tm, d), row),
        out_shape=jax.ShapeDtypeStruct((t, d), F32),
        scratch_shapes=[pltpu.VMEM((PEER_HEADS, N_KEYS, tm), F32),
                        pltpu.VMEM((PEER_HEADS, N_KEYS, tm), F32),
                        pltpu.VMEM((PEER_HEADS, tm), F32),
                        pltpu.VMEM((d, tm), F32),
                        pltpu.VMEM((2, eb, tm), F32),
                        pltpu.VMEM((eb, tm), BF16)],
        compiler_params=_params(("arbitrary", "arbitrary")),
        name="peer",
    )(sc, h2, u_bf, vt_bf, x1, mod3, final_g)


def kernel(x, c, w_mod, b_mod, norm1_g, w_in, b_f, w_pool, pool_scale, w_out, norm2_g,
           peer_w_query, peer_sub_keys, peer_u, peer_v, final_g):
    bsz, seq, d = x.shape
    depth = w_mod.shape[0]
    t = bsz * seq
    assert d == ATTN_HEADS * HEAD_DIM and seq % INPROJ_TM == 0 and t % PEER_TM == 0
    x2d = x.reshape(t, d)
    out = x2d
    for l in range(depth):
        mod3 = _mod_call(c, w_mod[l], b_mod[l]).reshape(bsz, N_MOD, d)

        w = w_in[l]
        o_q, o_k, o_v = POOL_W, POOL_W + d, POOL_W + 2 * d
        o_f = POOL_W + 3 * d
        o_g = o_f + ATTN_HEADS
        w_tok = jnp.concatenate([w[:, :o_q], w[:, o_k:o_v], w[:, o_g:]], axis=1).astype(BF16)
        w_feat_t = jnp.concatenate([w[:, o_q:o_k], w[:, o_v:o_f]], axis=1).T.astype(BF16)
        w_f = jnp.pad(w[:, o_f:o_g], ((0, 0), (0, LANES - ATTN_HEADS))).astype(BF16)
        b_f_row = jnp.pad(b_f[l], (0, LANES - ATTN_HEADS)).reshape(1, LANES)
        u, ka, ga, gb, qt, vt = _inproj_call(
            x2d, mod3, norm1_g[l].reshape(1, d), w_tok, w_feat_t, w_f, b_f_row,
            _forget_selectors(), seq)

        yb = _attn_call(qt, ka, vt, bsz, seq)

        n_hp = 2 * PEER_HEADS
        x1, h2, sc = _mix_call(
            u, ga, gb, yb, x2d, mod3, w_pool[l].astype(BF16), pool_scale[l].reshape(1, d),
            w_out[l].astype(BF16), norm2_g[l].reshape(1, d), peer_w_query[l].astype(BF16),
            peer_sub_keys[l].reshape(n_hp, N_KEYS, -1).astype(BF16), seq)

        assert l == depth - 1 == 0
        out = _peer_call(sc, h2, peer_u[l].astype(BF16), peer_v[l].T.astype(BF16),
                         x1, mod3, final_g.reshape(1, d), seq)
        x2d = out
    return out.reshape(bsz, seq, d)
```

```python
import functools

import jax
import jax.numpy as jnp
from jax import lax
from jax.experimental import pallas as pl
from jax.experimental.pallas import tpu as pltpu

F32 = jnp.float32
BF16 = jnp.bfloat16

EPS = 1e-6
NEG_INF = -1e30

POOL_WINDOWS = (2, 4, 8, 16)
POOL_GROUP_W = 128
POOL_W = len(POOL_WINDOWS) * POOL_GROUP_W
ATTN_HEADS = 16
HEAD_DIM = 64
PEER_HEADS = 8
N_KEYS = 128
PEER_TOPK = 16
N_MOD = 6

LANES = 128
VMEM_LIMIT = 56 * 1024 * 1024

INPROJ_TM = 512
ATTN_TQ = 512
ATTN_TK = 512
MIX_TM = 256
PEER_TM = 512
PEER_EB = 1024


def _params(sem, flags=None):
    return pltpu.CompilerParams(dimension_semantics=sem, vmem_limit_bytes=VMEM_LIMIT, flags=flags)


def _split3(a):
    hi = a.astype(BF16)
    r1 = a - hi.astype(F32)
    mid = r1.astype(BF16)
    lo = (r1 - mid.astype(F32)).astype(BF16)
    return hi, mid, lo


def _dot(a, b):
    return jnp.dot(a, b, preferred_element_type=F32)


def _dot_nt(a, b):
    return lax.dot_general(a, b, (((1,), (1,)), ((), ())), preferred_element_type=F32)


def _rms_mod(x, g, shift, scale):
    ms = jnp.mean(x * x, axis=-1, keepdims=True)
    y = x * lax.rsqrt(ms + EPS) * g
    return y * (1.0 + scale) + shift


def _mod_kernel(c_ref, w_ref, b_ref, o_ref):
    c_hi, c_mid, _ = _split3(c_ref[...])
    w_hi, w_mid, _ = _split3(w_ref[...])
    acc = _dot(c_hi, w_hi) + _dot(c_hi, w_mid) + _dot(c_mid, w_hi)
    o_ref[...] = acc + b_ref[...]


def _mod_call(c, w_mod, b_mod):
    bsz, d = c.shape
    n = w_mod.shape[1]
    bn = 1024
    return pl.pallas_call(
        _mod_kernel,
        grid=(n // bn,),
        in_specs=[pl.BlockSpec((bsz, d), lambda j: (0, 0)),
                  pl.BlockSpec((d, bn), lambda j: (0, j)),
                  pl.BlockSpec((1, bn), lambda j: (0, j))],
        out_specs=pl.BlockSpec((bsz, bn), lambda j: (0, j)),
        out_shape=jax.ShapeDtypeStruct((bsz, n), F32),
        compiler_params=_params(("parallel",)),
        name="mod",
    )(c, w_mod, b_mod.reshape(1, n))


def _inproj_kernel(x_ref, mod_ref, g_ref, w_ref, wt_ref, wf_ref, bf_ref, sel_ref,
                   u_ref, ka_ref, ga_ref, gb_ref, qt_ref, vt_ref,
                   carry_ref, *, tiles_per_seq, d_model):
    i = pl.program_id(0)
    tm = x_ref.shape[0]
    h = _rms_mod(x_ref[...], g_ref[...], mod_ref[0, 0:1, :], mod_ref[0, 1:2, :])
    hb = h.astype(BF16)

    cw = 512
    n_c = d_model // cw

    def proj(c0):
        return _dot(hb, w_ref[:, c0:c0 + cw])

    u_ref[...] = proj(0).astype(BF16)
    base = POOL_W
    pairs_per_chunk = cw // LANES
    for c in range(n_c):
        kc = proj(base + c * cw).astype(BF16)
        for pp in range(pairs_per_chunk):
            p = c * pairs_per_chunk + pp
            ka_ref[:, p * 2 * LANES:p * 2 * LANES + LANES] = kc[:, pp * LANES:(pp + 1) * LANES]
    base += d_model
    for c in range(n_c):
        ga_ref[:, c * cw:(c + 1) * cw] = jax.nn.sigmoid(proj(base + c * cw)).astype(BF16)
    base += d_model
    for c in range(n_c):
        gb_ref[:, c * cw:(c + 1) * cw] = jax.nn.sigmoid(proj(base + c * cw)).astype(BF16)

    scale = HEAD_DIM ** -0.5
    for c in range(n_c):
        qt_ref[c * cw:(c + 1) * cw, :] = (_dot_nt(wt_ref[c * cw:(c + 1) * cw, :], hb) * scale).astype(BF16)
    for c in range(n_c):
        r0 = d_model + c * cw
        vt_ref[c * cw:(c + 1) * cw, :] = _dot_nt(wt_ref[r0:r0 + cw, :], hb).astype(BF16)

    zf = _dot(hb, wf_ref[...]) + bf_ref[...]
    logf = jnp.minimum(zf, 0.0) - jnp.log1p(jnp.exp(-jnp.abs(zf)))
    row = lax.broadcasted_iota(jnp.int32, (tm, tm), 0)
    col = lax.broadcasted_iota(jnp.int32, (tm, tm), 1)
    tril = jnp.where(row >= col, 1.0, 0.0).astype(BF16)
    hi, mid, lo = _split3(logf)
    cs = _dot(tril, hi) + _dot(tril, mid) + _dot(tril, lo)

    @pl.when(i % tiles_per_seq == 0)
    def _():
        carry_ref[...] = jnp.zeros_like(carry_ref)

    f_cum = cs + carry_ref[0:1, :]
    carry_ref[...] = jnp.broadcast_to(f_cum[tm - 1:tm, :], carry_ref.shape)
    n_hi, n_mid, n_lo = _split3(-f_cum)
    faug = _dot(n_hi, sel_ref[0]) + _dot(n_mid, sel_ref[1]) + _dot(n_lo, sel_ref[2])
    for p in range(d_model // LANES):
        ka_ref[:, p * 2 * LANES + LANES:(p + 1) * 2 * LANES] = faug[:, p * LANES:(p + 1) * LANES].astype(BF16)


def _inproj_call(x2d, mod3, norm_g, w_tok, w_feat_t, w_f, b_f, sel, seq):
    t, d = x2d.shape
    tm = INPROJ_TM
    tps = seq // tm
    row = lambda i: (i, 0)
    colb = lambda i: (0, i)
    c2 = lambda i: (0, 0)
    out_shape = (
        jax.ShapeDtypeStruct((t, POOL_W), BF16),
        jax.ShapeDtypeStruct((t, 2 * d), BF16),
        jax.ShapeDtypeStruct((t, d), BF16), jax.ShapeDtypeStruct((t, d), BF16),
        jax.ShapeDtypeStruct((d, t), BF16), jax.ShapeDtypeStruct((d, t), BF16),
    )
    out_specs = (
        pl.BlockSpec((tm, POOL_W), row),
        pl.BlockSpec((tm, 2 * d), row),
        pl.BlockSpec((tm, d), row), pl.BlockSpec((tm, d), row),
        pl.BlockSpec((d, tm), colb), pl.BlockSpec((d, tm), colb),
    )
    return pl.pallas_call(
        functools.partial(_inproj_kernel, tiles_per_seq=tps, d_model=d),
        grid=(t // tm,),
        in_specs=[pl.BlockSpec((tm, d), row),
                  pl.BlockSpec((1, N_MOD, d), lambda i: (i // tps, 0, 0)),
                  pl.BlockSpec((1, d), c2),
                  pl.BlockSpec(w_tok.shape, c2),
                  pl.BlockSpec(w_feat_t.shape, c2),
                  pl.BlockSpec(w_f.shape, c2),
                  pl.BlockSpec(b_f.shape, c2),
                  pl.BlockSpec(sel.shape, lambda i: (0, 0, 0))],
        out_specs=out_specs,
        out_shape=out_shape,
        scratch_shapes=[pltpu.VMEM((8, LANES), F32)],
        compiler_params=_params(("arbitrary",)),
        name="inproj",
    )(x2d, mod3, norm_g, w_tok, w_feat_t, w_f, b_f, sel)


def _forget_selectors():
    h = jnp.arange(LANES)[:, None]
    c = jnp.arange(ATTN_HEADS // 2 * LANES)[None, :]
    sels = []
    for r in range(3):
        hit = (h < ATTN_HEADS) & (c == (h // 2) * LANES + 3 * (h % 2) + r)
        sels.append(jnp.where(hit, 1.0, 0.0))
    return jnp.stack(sels).astype(BF16)


def _attn_kernel(qt_ref, ka_ref, vt_ref, o_ref, *, tq, tk):
    seq = ka_ref.shape[0]
    rowid = lax.broadcasted_iota(jnp.int32, (LANES, tq), 0)
    krow = lax.broadcasted_iota(jnp.int32, (tk, tq), 0)
    qcol = lax.broadcasted_iota(jnp.int32, (tk, tq), 1)
    aug = [jnp.where((rowid >= 3 * h) & (rowid < 3 * h + 3), 1.0, 0.0).astype(BF16) for h in range(2)]
    head = [jnp.where(rowid // HEAD_DIM == h, 1.0, 0.0).astype(BF16) for h in range(2)]

    def q_tile(qi, carry):
        q0 = pl.multiple_of(qi * tq, tq)
        qt = qt_ref[:, pl.ds(q0, tq)]
        ws = [jnp.concatenate([qt * head[h], aug[h]], axis=0) for h in range(2)]

        def kv_step(j, state, diag):
            k0 = pl.multiple_of(j * tk, tk)
            ka = ka_ref[pl.ds(k0, tk), :]
            new = []
            for h in range(2):
                m, l, acc = state[h]
                s = _dot(ka, ws[h])
                if diag is not None:
                    s = jnp.where(krow + diag * tk <= qcol, s, NEG_INF)
                m_new = jnp.maximum(m, jnp.max(s, axis=0, keepdims=True))
                alpha = jnp.exp(m - m_new)
                p = jnp.exp(s - m_new)
                l = alpha * l + jnp.sum(p, axis=0, keepdims=True)
                vt = vt_ref[h * HEAD_DIM:(h + 1) * HEAD_DIM, pl.ds(k0, tk)]
                acc = alpha * acc + _dot(vt, p.astype(BF16))
                new.append((m_new, l, acc))
            return tuple(new)

        init = tuple((jnp.full((1, tq), NEG_INF, F32), jnp.zeros((1, tq), F32),
                      jnp.zeros((HEAD_DIM, tq), F32)) for _ in range(2))
        n_full = qi * (tq // tk)
        state = lax.fori_loop(0, n_full, lambda j, st: kv_step(j, st, None), init)
        for dblk in range(tq // tk):
            state = kv_step(n_full + dblk, state, dblk)
        out_t = jnp.concatenate([state[0][2] / state[0][1], state[1][2] / state[1][1]], axis=0)
        o_ref[pl.ds(q0, tq), :] = out_t.T.astype(o_ref.dtype)
        return carry

    lax.fori_loop(0, seq // tq, q_tile, 0)


def _attn_call(qt, ka, vt, bsz, seq):
    d, t = qt.shape
    n_pairs = d // LANES
    return pl.pallas_call(
        functools.partial(_attn_kernel, tq=ATTN_TQ, tk=ATTN_TK),
        grid=(bsz, n_pairs),
        in_specs=[pl.BlockSpec((LANES, seq), lambda b, p: (p, b)),
                  pl.BlockSpec((seq, 2 * LANES), lambda b, p: (b, p)),
                  pl.BlockSpec((LANES, seq), lambda b, p: (p, b))],
        out_specs=pl.BlockSpec((seq, LANES), lambda b, p: (b, p)),
        out_shape=jax.ShapeDtypeStruct((t, d), BF16),
        compiler_params=_params(("parallel", "parallel")),
        name="attn",
    )(qt, ka, vt)


def _mix_kernel(uc_ref, up_ref, ga_ref, gb_ref, yb_ref, x_ref, mod_ref, wp_ref, ps_ref,
                wo_ref, g2_ref, wq_ref, sk_ref,
                x1_ref, h2_ref, sc_ref, y_ref, *, tiles_per_seq):
    i = pl.program_id(0)
    tm = x_ref.shape[0]
    halo = LANES
    first = (i % tiles_per_seq) == 0
    pos0 = (i % tiles_per_seq) * tm
    r_d = lax.broadcasted_iota(jnp.int32, (tm, tm), 0)
    c_d = lax.broadcasted_iota(jnp.int32, (tm, tm), 1)
    r_o = lax.broadcasted_iota(jnp.int32, (tm, halo), 0)
    c_o = lax.broadcasted_iota(jnp.int32, (tm, halo), 1)
    pos = pos0 + lax.broadcasted_iota(jnp.int32, (tm, 1), 0)
    gw = wp_ref.shape[2]
    for g, w in enumerate(POOL_WINDOWS):
        lo, hi = g * POOL_GROUP_W, (g + 1) * POOL_GROUP_W
        u_cur = uc_ref[:, lo:hi]
        u_prev = up_ref[tm - halo:tm, lo:hi]
        lag = r_d - c_d
        band_d = jnp.where((lag >= 0) & (lag < w), 1.0, 0.0).astype(BF16)
        band_o = jnp.where((r_o + halo - c_o < w) & jnp.logical_not(first), 1.0, 0.0).astype(BF16)
        wsum = _dot(band_d, u_cur) + _dot(band_o, u_prev)
        cnt = jnp.minimum(pos + 1, w).astype(F32)
        pooled = wsum / cnt - u_cur.astype(F32)
        ya = _dot(pooled.astype(BF16), wp_ref[g]) * ps_ref[:, g * gw:(g + 1) * gw]
        sl = slice(g * gw, (g + 1) * gw)
        y = ga_ref[:, sl].astype(F32) * ya + gb_ref[:, sl].astype(F32) * yb_ref[:, sl].astype(F32)
        y_ref[:, sl] = y.astype(BF16)

    o = _dot(y_ref[...], wo_ref[...])
    x1 = x_ref[...] + mod_ref[0, 2:3, :] * o
    x1_ref[...] = x1
    h2 = _rms_mod(x1, g2_ref[...], mod_ref[0, 3:4, :], mod_ref[0, 4:5, :]).astype(BF16)
    h2_ref[...] = h2
    n_hp = sk_ref.shape[0]
    half = sk_ref.shape[2]
    qw = 512
    per = qw // half
    for c in range(n_hp // per):
        qp = _dot(h2, wq_ref[:, c * qw:(c + 1) * qw]).astype(BF16)
        for k in range(per):
            hp = c * per + k
            sc_ref[hp] = _dot_nt(sk_ref[hp], qp[:, k * half:(k + 1) * half])


def _mix_call(u, ga, gb, yb, x2d, mod3, w_pool, pool_scale, w_out, norm2_g, w_query, sub_keys, seq):
    t, d = x2d.shape
    tm = MIX_TM
    tps = seq // tm
    row = lambda i: (i, 0)
    c2 = lambda i: (0, 0)
    c3 = lambda i: (0, 0, 0)
    n_hp = sub_keys.shape[0]
    return pl.pallas_call(
        functools.partial(_mix_kernel, tiles_per_seq=tps),
        grid=(t // tm,),
        in_specs=[pl.BlockSpec((tm, POOL_W), row),
                  pl.BlockSpec((tm, POOL_W), lambda i: (jnp.maximum(i - 1, 0), 0)),
                  pl.BlockSpec((tm, d), row), pl.BlockSpec((tm, d), row), pl.BlockSpec((tm, d), row),
                  pl.BlockSpec((tm, d), row),
                  pl.BlockSpec((1, N_MOD, d), lambda i: (i // tps, 0, 0)),
                  pl.BlockSpec(w_pool.shape, c3),
                  pl.BlockSpec((1, d), c2),
                  pl.BlockSpec((d, d), c2),
                  pl.BlockSpec((1, d), c2),
                  pl.BlockSpec(w_query.shape, c2),
                  pl.BlockSpec(sub_keys.shape, c3)],
        out_specs=(pl.BlockSpec((tm, d), row), pl.BlockSpec((tm, d), row),
                   pl.BlockSpec((n_hp, N_KEYS, tm), lambda i: (0, 0, i))),
        out_shape=(jax.ShapeDtypeStruct((t, d), F32), jax.ShapeDtypeStruct((t, d), BF16),
                   jax.ShapeDtypeStruct((n_hp, N_KEYS, t), F32)),
        scratch_shapes=[pltpu.VMEM((tm, d), BF16)],
        compiler_params=_params(("parallel",)),
        name="mix",
    )(u, u, ga, gb, yb, x2d, mod3, w_pool, pool_scale, w_out, norm2_g, w_query, sub_keys)


RANK_CODE_BASE = -(2.0 ** 127)


def _top16_rows(s):
    cur = s
    rows = []
    for r in range(PEER_TOPK):
        m = jnp.max(cur, axis=0, keepdims=True)
        rows.append(m)
        cur = jnp.where(cur == m, RANK_CODE_BASE * (1.0 + r / 64.0), cur)
    bits = lax.bitcast_convert_type(cur, jnp.int32)
    coded = ((bits >> 17) & 63).astype(F32)
    rank = jnp.where(cur <= RANK_CODE_BASE, coded, float(PEER_TOPK))
    return rows, rank


def _stack_rows(rows, tm):
    n = len(rows)
    rid = lax.broadcasted_iota(jnp.int32, (n, tm), 0)
    arr = jnp.zeros((n, tm), F32)
    for r, v in enumerate(rows):
        arr = jnp.where(rid == r, v, arr)
    return arr


def _candidates(rows1, arr1_hi, rows2, arr2, combine):
    pieces = [combine(rows1[0], arr2)]
    for a in range(1, 8):
        pieces.append(combine(rows1[a], arr2[0:8]))
    pieces.append(combine(arr1_hi, rows2[0]))
    return jnp.concatenate(pieces, axis=0)


def _peer_stats(sc_ref, e1_ref, lim_ref, e2_ref, r2_ref, h):
    tm = sc_ref.shape[2]
    s1 = sc_ref[2 * h]
    s2 = sc_ref[2 * h + 1]
    t1, rank1 = _top16_rows(s1)
    t2, rank2 = _top16_rows(s2)
    t1_hi = _stack_rows(t1[8:], tm)
    t2_arr = _stack_rows(t2, tm)
    cand = _candidates(t1, t1_hi, t2, t2_arr, lambda a, b: a + b)
    n_c = cand.shape[0]
    cid = lax.broadcasted_iota(jnp.int32, (n_c, tm), 0)
    cur = cand
    z = jnp.zeros((1, tm), F32)
    m0 = None
    for r in range(PEER_TOPK):
        m = jnp.max(cur, axis=0, keepdims=True)
        if r == 0:
            m0 = m
        first = jnp.min(jnp.where(cur == m, cid, n_c), axis=0, keepdims=True)
        cur = jnp.where(cid == first, -jnp.inf, cur)
        z = z + jnp.exp(m - m0)
    sel = jnp.where(cur == -jnp.inf, 1.0, 0.0)
    width = [jnp.sum(sel[0:16], axis=0, keepdims=True)]
    for a in range(1, 8):
        width.append(jnp.sum(sel[8 + 8 * a:16 + 8 * a], axis=0, keepdims=True))
    for a in range(8, PEER_TOPK):
        width.append(sel[64 + a:65 + a])
    lim = jnp.zeros_like(s1)
    for a in range(PEER_TOPK):
        lim = jnp.where(rank1 == float(a), width[a], lim)
    inv_z = 1.0 / z
    e1_ref[h] = jnp.exp(s1 - t1[0]) * inv_z
    lim_ref[h] = lim
    e2_ref[h] = jnp.exp(s2 - t2[0]).astype(BF16)
    r2_ref[h] = rank2.astype(BF16)


def _peer_kernel(sc_ref, h2_ref, u_ref, vt_ref, x1_ref, mod_ref, fg_ref, o_ref,
                 e1_ref, lim_ref, e2_ref, r2_ref, acc_ref, a0_ref, a1_ref, y_ref):
    i = pl.program_id(0)
    s = pl.program_id(1)
    n_blk = pl.num_programs(1) - 1
    n_heads = e1_ref.shape[0]
    eb = u_ref.shape[0]
    rows_per_blk = eb // N_KEYS
    kc = 2 * N_KEYS

    @pl.when((i == 0) & (s == 0))
    def _():
        a1_ref[...] = jnp.zeros_like(a1_ref)
        acc_ref[...] = jnp.zeros_like(acc_ref)

    @pl.when(s == 0)
    def _():
        def head(h, carry):
            _peer_stats(sc_ref, e1_ref, lim_ref, e2_ref, r2_ref, h)
            return carry
        lax.fori_loop(0, n_heads, head, 0)

    def step(aw_ref, ar_ref):
        aw_ref[...] = _dot_nt(u_ref[...], h2_ref[...])
        tm = ar_ref.shape[1]
        blk = jnp.maximum(s - 1, 0)
        for ii in range(rows_per_blk):
            i_key = blk * rows_per_blk + ii
            g = jnp.zeros((N_KEYS, tm), BF16)
            for h in range(n_heads):
                p = e1_ref[h, pl.ds(i_key, 1), :].astype(BF16) * e2_ref[h]
                lim = lim_ref[h, pl.ds(i_key, 1), :].astype(BF16)
                g = g + jnp.where(r2_ref[h] < lim, p, jnp.zeros_like(p))
            a = ar_ref[ii * N_KEYS:(ii + 1) * N_KEYS, :]
            act = 0.5 * a * (1.0 + lax.erf(a * (0.5 ** 0.5)))
            y_ref[ii * N_KEYS:(ii + 1) * N_KEYS, :] = act.astype(BF16) * g
            if (ii + 1) * N_KEYS % kc == 0:
                c0 = (ii + 1) * N_KEYS - kc
                acc_ref[...] += _dot(vt_ref[:, c0:c0 + kc], y_ref[c0:c0 + kc, :])

    @pl.when(s % 2 == 0)
    def _():
        step(a0_ref, a1_ref)

    @pl.when(s % 2 == 1)
    def _():
        step(a1_ref, a0_ref)

    @pl.when(s == 0)
    def _():
        acc_ref[...] = jnp.zeros_like(acc_ref)

    @pl.when(s == n_blk)
    def _():
        peer = acc_ref[...].T
        x2 = x1_ref[...] + mod_ref[0, 5:6, :] * peer
        ms = jnp.mean(x2 * x2, axis=-1, keepdims=True)
        o_ref[...] = x2 * lax.rsqrt(ms + EPS) * fg_ref[...]


def _peer_call(sc, h2, u_bf, vt_bf, x1, mod3, final_g, seq):
    t, d = x1.shape
    tm, eb = PEER_TM, PEER_EB
    tps = seq // tm
    n_hp = sc.shape[0]
    n_exp = u_bf.shape[0]
    row = lambda i, j: (i, 0)
    n_blk = n_exp // eb
    return pl.pallas_call(
        _peer_kernel,
        grid=(t // tm, n_blk + 1),
        in_specs=[pl.BlockSpec((n_hp, N_KEYS, tm), lambda i, j: (0, 0, i)),
                  pl.BlockSpec((tm, d), row),
                  pl.BlockSpec((eb, d), lambda i, j: (jnp.minimum(j, n_blk - 1), 0)),
                  pl.BlockSpec((d, eb), lambda i, j: (0, jnp.maximum(j - 1, 0))),
                  pl.BlockSpec((tm, d), row),
                  pl.BlockSpec((1, N_MOD, d), lambda i, j: (i // tps, 0, 0)),
                  pl.BlockSpec((1, d), lambda i, j: (0, 0))],
        out_specs=pl.BlockSpec((tm, d), row),
        out_shape=jax.ShapeDtypeStruct((t, d), F32),
        scratch_shapes=[pltpu.VMEM((PEER_HEADS, N_KEYS, tm), F32),
                        pltpu.VMEM((PEER_HEADS, N_KEYS, tm), F32),
                        pltpu.VMEM((PEER_HEADS, N_KEYS, tm), BF16),
                        pltpu.VMEM((PEER_HEADS, N_KEYS, tm), BF16),
                        pltpu.VMEM((d, tm), F32),
                        pltpu.VMEM((eb, tm), F32),
                        pltpu.VMEM((eb, tm), F32),
                        pltpu.VMEM((eb, tm), BF16)],
        compiler_params=_params(("arbitrary", "arbitrary")),
        name="peer",
    )(sc, h2, u_bf, vt_bf, x1, mod3, final_g)


def kernel(x, c, w_mod, b_mod, norm1_g, w_in, b_f, w_pool, pool_scale, w_out, norm2_g,
           peer_w_query, peer_sub_keys, peer_u, peer_v, final_g):
    bsz, seq, d = x.shape
    depth = w_mod.shape[0]
    t = bsz * seq
    assert d == ATTN_HEADS * HEAD_DIM and seq % INPROJ_TM == 0 and t % PEER_TM == 0
    x2d = x.reshape(t, d)
    out = x2d
    for l in range(depth):
        mod3 = _mod_call(c, w_mod[l], b_mod[l]).reshape(bsz, N_MOD, d)

        w = w_in[l]
        o_q, o_k, o_v = POOL_W, POOL_W + d, POOL_W + 2 * d
        o_f = POOL_W + 3 * d
        o_g = o_f + ATTN_HEADS
        w_tok = jnp.concatenate([w[:, :o_q], w[:, o_k:o_v], w[:, o_g:]], axis=1).astype(BF16)
        w_feat_t = jnp.concatenate([w[:, o_q:o_k], w[:, o_v:o_f]], axis=1).T.astype(BF16)
        w_f = jnp.pad(w[:, o_f:o_g], ((0, 0), (0, LANES - ATTN_HEADS))).astype(BF16)
        b_f_row = jnp.pad(b_f[l], (0, LANES - ATTN_HEADS)).reshape(1, LANES)
        u, ka, ga, gb, qt, vt = _inproj_call(
            x2d, mod3, norm1_g[l].reshape(1, d), w_tok, w_feat_t, w_f, b_f_row,
            _forget_selectors(), seq)

        yb = _attn_call(qt, ka, vt, bsz, seq)

        n_hp = 2 * PEER_HEADS
        x1, h2, sc = _mix_call(
            u, ga, gb, yb, x2d, mod3, w_pool[l].astype(BF16), pool_scale[l].reshape(1, d),
            w_out[l].astype(BF16), norm2_g[l].reshape(1, d), peer_w_query[l].astype(BF16),
            peer_sub_keys[l].reshape(n_hp, N_KEYS, -1).astype(BF16), seq)

        assert l == depth - 1 == 0
        out = _peer_call(sc, h2, peer_u[l].astype(BF16), peer_v[l].T.astype(BF16),
                         x1, mod3, final_g.reshape(1, d), seq)
        x2d = out
    return out.reshape(bsz, seq, d)
```

```python
import functools

import jax
import jax.numpy as jnp
from jax import lax
from jax.experimental import pallas as pl
from jax.experimental.pallas import tpu as pltpu

F32 = jnp.float32
BF16 = jnp.bfloat16

EPS = 1e-6
NEG_INF = -1e30

POOL_WINDOWS = (2, 4, 8, 16)
POOL_GROUP_W = 128
POOL_W = len(POOL_WINDOWS) * POOL_GROUP_W
ATTN_HEADS = 16
HEAD_DIM = 64
PEER_HEADS = 8
N_KEYS = 128
PEER_TOPK = 16
N_MOD = 6

LANES = 128
VMEM_LIMIT = 56 * 1024 * 1024

INPROJ_TM = 512
ATTN_TQ = 512
ATTN_TK = 512
MIX_TM = 256
PEER_TM = 512
PEER_EB = 1024


def _params(sem, flags=None):
    return pltpu.CompilerParams(dimension_semantics=sem, vmem_limit_bytes=VMEM_LIMIT, flags=flags)


def _split3(a):
    hi = a.astype(BF16)
    r1 = a - hi.astype(F32)
    mid = r1.astype(BF16)
    lo = (r1 - mid.astype(F32)).astype(BF16)
    return hi, mid, lo


def _dot(a, b):
    return jnp.dot(a, b, preferred_element_type=F32)


def _dot_nt(a, b):
    return lax.dot_general(a, b, (((1,), (1,)), ((), ())), preferred_element_type=F32)


def _rms_mod(x, g, shift, scale):
    ms = jnp.mean(x * x, axis=-1, keepdims=True)
    y = x * lax.rsqrt(ms + EPS) * g
    return y * (1.0 + scale) + shift


def _mod_kernel(c_ref, w_ref, b_ref, o_ref):
    c_hi, c_mid, _ = _split3(c_ref[...])
    w_hi, w_mid, _ = _split3(w_ref[...])
    acc = _dot(c_hi, w_hi) + _dot(c_hi, w_mid) + _dot(c_mid, w_hi)
    o_ref[...] = acc + b_ref[...]


def _mod_call(c, w_mod, b_mod):
    bsz, d = c.shape
    n = w_mod.shape[1]
    bn = 1024
    return pl.pallas_call(
        _mod_kernel,
        grid=(n // bn,),
        in_specs=[pl.BlockSpec((bsz, d), lambda j: (0, 0)),
                  pl.BlockSpec((d, bn), lambda j: (0, j)),
                  pl.BlockSpec((1, bn), lambda j: (0, j))],
        out_specs=pl.BlockSpec((bsz, bn), lambda j: (0, j)),
        out_shape=jax.ShapeDtypeStruct((bsz, n), F32),
        compiler_params=_params(("parallel",)),
        name="mod",
    )(c, w_mod, b_mod.reshape(1, n))


def _inproj_kernel(x_ref, mod_ref, g_ref, w_ref, wt_ref, wf_ref, bf_ref, sel_ref,
                   u_ref, ka_ref, ga_ref, gb_ref, qt_ref, vt_ref,
                   carry_ref, *, tiles_per_seq, d_model):
    i = pl.program_id(0)
    tm = x_ref.shape[0]
    h = _rms_mod(x_ref[...], g_ref[...], mod_ref[0, 0:1, :], mod_ref[0, 1:2, :])
    hb = h.astype(BF16)

    cw = 512
    n_c = d_model // cw

    def proj(c0):
        return _dot(hb, w_ref[:, c0:c0 + cw])

    u_ref[...] = proj(0).astype(BF16)
    base = POOL_W
    pairs_per_chunk = cw // LANES
    for c in range(n_c):
        kc = proj(base + c * cw).astype(BF16)
        for pp in range(pairs_per_chunk):
            p = c * pairs_per_chunk + pp
            ka_ref[:, p * 2 * LANES:p * 2 * LANES + LANES] = kc[:, pp * LANES:(pp + 1) * LANES]
    base += d_model
    for c in range(n_c):
        ga_ref[:, c * cw:(c + 1) * cw] = jax.nn.sigmoid(proj(base + c * cw)).astype(BF16)
    base += d_model
    for c in range(n_c):
        gb_ref[:, c * cw:(c + 1) * cw] = jax.nn.sigmoid(proj(base + c * cw)).astype(BF16)

    scale = HEAD_DIM ** -0.5
    for c in range(n_c):
        qt_ref[c * cw:(c + 1) * cw, :] = (_dot_nt(wt_ref[c * cw:(c + 1) * cw, :], hb) * scale).astype(BF16)
    for c in range(n_c):
        r0 = d_model + c * cw
        vt_ref[c * cw:(c + 1) * cw, :] = _dot_nt(wt_ref[r0:r0 + cw, :], hb).astype(BF16)

    zf = _dot(hb, wf_ref[...]) + bf_ref[...]
    logf = jnp.minimum(zf, 0.0) - jnp.log1p(jnp.exp(-jnp.abs(zf)))
    row = lax.broadcasted_iota(jnp.int32, (tm, tm), 0)
    col = lax.broadcasted_iota(jnp.int32, (tm, tm), 1)
    tril = jnp.where(row >= col, 1.0, 0.0).astype(BF16)
    hi, mid, lo = _split3(logf)
    cs = _dot(tril, hi) + _dot(tril, mid) + _dot(tril, lo)

    @pl.when(i % tiles_per_seq == 0)
    def _():
        carry_ref[...] = jnp.zeros_like(carry_ref)

    f_cum = cs + carry_ref[0:1, :]
    carry_ref[...] = jnp.broadcast_to(f_cum[tm - 1:tm, :], carry_ref.shape)
    n_hi, n_mid, n_lo = _split3(-f_cum)
    faug = _dot(n_hi, sel_ref[0]) + _dot(n_mid, sel_ref[1]) + _dot(n_lo, sel_ref[2])
    for p in range(d_model // LANES):
        ka_ref[:, p * 2 * LANES + LANES:(p + 1) * 2 * LANES] = faug[:, p * LANES:(p + 1) * LANES].astype(BF16)


def _inproj_call(x2d, mod3, norm_g, w_tok, w_feat_t, w_f, b_f, sel, seq):
    t, d = x2d.shape
    tm = INPROJ_TM
    tps = seq // tm
    row = lambda i: (i, 0)
    colb = lambda i: (0, i)
    c2 = lambda i: (0, 0)
    out_shape = (
        jax.ShapeDtypeStruct((t, POOL_W), BF16),
        jax.ShapeDtypeStruct((t, 2 * d), BF16),
        jax.ShapeDtypeStruct((t, d), BF16), jax.ShapeDtypeStruct((t, d), BF16),
        jax.ShapeDtypeStruct((d, t), BF16), jax.ShapeDtypeStruct((d, t), BF16),
    )
    out_specs = (
        pl.BlockSpec((tm, POOL_W), row),
        pl.BlockSpec((tm, 2 * d), row),
        pl.BlockSpec((tm, d), row), pl.BlockSpec((tm, d), row),
        pl.BlockSpec((d, tm), colb), pl.BlockSpec((d, tm), colb),
    )
    return pl.pallas_call(
        functools.partial(_inproj_kernel, tiles_per_seq=tps, d_model=d),
        grid=(t // tm,),
        in_specs=[pl.BlockSpec((tm, d), row),
                  pl.BlockSpec((1, N_MOD, d), lambda i: (i // tps, 0, 0)),
                  pl.BlockSpec((1, d), c2),
                  pl.BlockSpec(w_tok.shape, c2),
                  pl.BlockSpec(w_feat_t.shape, c2),
                  pl.BlockSpec(w_f.shape, c2),
                  pl.BlockSpec(b_f.shape, c2),
                  pl.BlockSpec(sel.shape, lambda i: (0, 0, 0))],
        out_specs=out_specs,
        out_shape=out_shape,
        scratch_shapes=[pltpu.VMEM((8, LANES), F32)],
        compiler_params=_params(("arbitrary",)),
        name="inproj",
    )(x2d, mod3, norm_g, w_tok, w_feat_t, w_f, b_f, sel)


def _forget_selectors():
    h = jnp.arange(LANES)[:, None]
    c = jnp.arange(ATTN_HEADS // 2 * LANES)[None, :]
    sels = []
    for r in range(3):
        hit = (h < ATTN_HEADS) & (c == (h // 2) * LANES + 3 * (h % 2) + r)
        sels.append(jnp.where(hit, 1.0, 0.0))
    return jnp.stack(sels).astype(BF16)


def _attn_kernel(qt_ref, ka_ref, vt_ref, o_ref, *, tq, tk):
    seq = ka_ref.shape[0]
    rowid = lax.broadcasted_iota(jnp.int32, (LANES, tq), 0)
    krow = lax.broadcasted_iota(jnp.int32, (tk, tq), 0)
    qcol = lax.broadcasted_iota(jnp.int32, (tk, tq), 1)
    aug = [jnp.where((rowid >= 3 * h) & (rowid < 3 * h + 3), 1.0, 0.0).astype(BF16) for h in range(2)]
    head = [jnp.where(rowid // HEAD_DIM == h, 1.0, 0.0).astype(BF16) for h in range(2)]

    def q_tile(qi, carry):
        q0 = pl.multiple_of(qi * tq, tq)
        qt = qt_ref[:, pl.ds(q0, tq)]
        ws = [jnp.concatenate([qt * head[h], aug[h]], axis=0) for h in range(2)]

        def kv_step(j, state, diag):
            k0 = pl.multiple_of(j * tk, tk)
            ka = ka_ref[pl.ds(k0, tk), :]
            new = []
            for h in range(2):
                m, l, acc = state[h]
                s = _dot(ka, ws[h])
                if diag is not None:
                    s = jnp.where(krow + diag * tk <= qcol, s, NEG_INF)
                m_new = jnp.maximum(m, jnp.max(s, axis=0, keepdims=True))
                alpha = jnp.exp(m - m_new)
                p = jnp.exp(s - m_new)
                l = alpha * l + jnp.sum(p, axis=0, keepdims=True)
                vt = vt_ref[h * HEAD_DIM:(h + 1) * HEAD_DIM, pl.ds(k0, tk)]
                acc = alpha * acc + _dot(vt, p.astype(BF16))
                new.append((m_new, l, acc))
            return tuple(new)

        init = tuple((jnp.full((1, tq), NEG_INF, F32), jnp.zeros((1, tq), F32),
                      jnp.zeros((HEAD_DIM, tq), F32)) for _ in range(2))
        n_full = qi * (tq // tk)
        state = lax.fori_loop(0, n_full, lambda j, st: kv_step(j, st, None), init)
        for dblk in range(tq // tk):
            state = kv_step(n_full + dblk, state, dblk)
        out_t = jnp.concatenate([state[0][2] / state[0][1], state[1][2] / state[1][1]], axis=0)
        o_ref[pl.ds(q0, tq), :] = out_t.T.astype(o_ref.dtype)
        return carry

    lax.fori_loop(0, seq // tq, q_tile, 0)


def _attn_call(qt, ka, vt, bsz, seq):
    d, t = qt.shape
    n_pairs = d // LANES
    return pl.pallas_call(
        functools.partial(_attn_kernel, tq=ATTN_TQ, tk=ATTN_TK),
        grid=(bsz, n_pairs),
        in_specs=[pl.BlockSpec((LANES, seq), lambda b, p: (p, b)),
                  pl.BlockSpec((seq, 2 * LANES), lambda b, p: (b, p)),
                  pl.BlockSpec((LANES, seq), lambda b, p: (p, b))],
        out_specs=pl.BlockSpec((seq, LANES), lambda b, p: (b, p)),
        out_shape=jax.ShapeDtypeStruct((t, d), BF16),
        compiler_params=_params(("parallel", "parallel")),
        name="attn",
    )(qt, ka, vt)


def _mix_kernel(uc_ref, up_ref, ga_ref, gb_ref, yb_ref, x_ref, mod_ref, wp_ref, ps_ref,
                wo_ref, g2_ref, wq_ref, sk_ref,
                x1_ref, h2_ref, sc_ref, y_ref, *, tiles_per_seq):
    i = pl.program_id(0)
    tm = x_ref.shape[0]
    halo = LANES
    first = (i % tiles_per_seq) == 0
    pos0 = (i % tiles_per_seq) * tm
    r_d = lax.broadcasted_iota(jnp.int32, (tm, tm), 0)
    c_d = lax.broadcasted_iota(jnp.int32, (tm, tm), 1)
    r_o = lax.broadcasted_iota(jnp.int32, (tm, halo), 0)
    c_o = lax.broadcasted_iota(jnp.int32, (tm, halo), 1)
    pos = pos0 + lax.broadcasted_iota(jnp.int32, (tm, 1), 0)
    gw = wp_ref.shape[2]
    for g, w in enumerate(POOL_WINDOWS):
        lo, hi = g * POOL_GROUP_W, (g + 1) * POOL_GROUP_W
        u_cur = uc_ref[:, lo:hi]
        u_prev = up_ref[tm - halo:tm, lo:hi]
        lag = r_d - c_d
        band_d = jnp.where((lag >= 0) & (lag < w), 1.0, 0.0).astype(BF16)
        band_o = jnp.where((r_o + halo - c_o < w) & jnp.logical_not(first), 1.0, 0.0).astype(BF16)
        wsum = _dot(band_d, u_cur) + _dot(band_o, u_prev)
        cnt = jnp.minimum(pos + 1, w).astype(F32)
        pooled = wsum / cnt - u_cur.astype(F32)
        ya = _dot(pooled.astype(BF16), wp_ref[g]) * ps_ref[:, g * gw:(g + 1) * gw]
        sl = slice(g * gw, (g + 1) * gw)
        y = ga_ref[:, sl].astype(F32) * ya + gb_ref[:, sl].astype(F32) * yb_ref[:, sl].astype(F32)
        y_ref[:, sl] = y.astype(BF16)

    o = _dot(y_ref[...], wo_ref[...])
    x1 = x_ref[...] + mod_ref[0, 2:3, :] * o
    x1_ref[...] = x1
    h2f = _rms_mod(x1, g2_ref[...], mod_ref[0, 3:4, :], mod_ref[0, 4:5, :])
    h2 = h2f.astype(BF16)
    h2_ref[...] = h2f.T.astype(BF16)
    n_hp = sk_ref.shape[0]
    half = sk_ref.shape[2]
    qw = 512
    per = qw // half
    for c in range(n_hp // per):
        qp = _dot(h2, wq_ref[:, c * qw:(c + 1) * qw]).astype(BF16)
        for k in range(per):
            hp = c * per + k
            sc_ref[hp] = _dot_nt(sk_ref[hp], qp[:, k * half:(k + 1) * half])


def _mix_call(u, ga, gb, yb, x2d, mod3, w_pool, pool_scale, w_out, norm2_g, w_query, sub_keys, seq):
    t, d = x2d.shape
    tm = MIX_TM
    tps = seq // tm
    row = lambda i: (i, 0)
    c2 = lambda i: (0, 0)
    c3 = lambda i: (0, 0, 0)
    n_hp = sub_keys.shape[0]
    return pl.pallas_call(
        functools.partial(_mix_kernel, tiles_per_seq=tps),
        grid=(t // tm,),
        in_specs=[pl.BlockSpec((tm, POOL_W), row),
                  pl.BlockSpec((tm, POOL_W), lambda i: (jnp.maximum(i - 1, 0), 0)),
                  pl.BlockSpec((tm, d), row), pl.BlockSpec((tm, d), row), pl.BlockSpec((tm, d), row),
                  pl.BlockSpec((tm, d), row),
                  pl.BlockSpec((1, N_MOD, d), lambda i: (i // tps, 0, 0)),
                  pl.BlockSpec(w_pool.shape, c3),
                  pl.BlockSpec((1, d), c2),
                  pl.BlockSpec((d, d), c2),
                  pl.BlockSpec((1, d), c2),
                  pl.BlockSpec(w_query.shape, c2),
                  pl.BlockSpec(sub_keys.shape, c3)],
        out_specs=(pl.BlockSpec((tm, d), row), pl.BlockSpec((d, tm), lambda i: (0, i)),
                   pl.BlockSpec((n_hp, N_KEYS, tm), lambda i: (0, 0, i))),
        out_shape=(jax.ShapeDtypeStruct((t, d), F32), jax.ShapeDtypeStruct((d, t), BF16),
                   jax.ShapeDtypeStruct((n_hp, N_KEYS, t), F32)),
        scratch_shapes=[pltpu.VMEM((tm, d), BF16)],
        compiler_params=_params(("parallel",)),
        name="mix",
    )(u, u, ga, gb, yb, x2d, mod3, w_pool, pool_scale, w_out, norm2_g, w_query, sub_keys)


RANK_CODE_BASE = -(2.0 ** 127)


def _top16_rows(s):
    cur = s
    rows = []
    for r in range(PEER_TOPK):
        m = jnp.max(cur, axis=0, keepdims=True)
        rows.append(m)
        cur = jnp.where(cur == m, RANK_CODE_BASE * (1.0 + r / 64.0), cur)
    bits = lax.bitcast_convert_type(cur, jnp.int32)
    coded = ((bits >> 17) & 63).astype(F32)
    rank = jnp.where(cur <= RANK_CODE_BASE, coded, float(PEER_TOPK))
    return rows, rank


def _stack_rows(rows, tm):
    n = len(rows)
    rid = lax.broadcasted_iota(jnp.int32, (n, tm), 0)
    arr = jnp.zeros((n, tm), F32)
    for r, v in enumerate(rows):
        arr = jnp.where(rid == r, v, arr)
    return arr


def _candidates(rows1, arr1_hi, rows2, arr2, combine):
    pieces = [combine(rows1[0], arr2)]
    for a in range(1, 8):
        pieces.append(combine(rows1[a], arr2[0:8]))
    pieces.append(combine(arr1_hi, rows2[0]))
    return jnp.concatenate(pieces, axis=0)


def _peer_stats(sc_ref, e1_ref, lim_ref, e2_ref, r2_ref, h):
    tm = sc_ref.shape[2]
    s1 = sc_ref[2 * h]
    s2 = sc_ref[2 * h + 1]
    t1, rank1 = _top16_rows(s1)
    t2, rank2 = _top16_rows(s2)
    t1_hi = _stack_rows(t1[8:], tm)
    t2_arr = _stack_rows(t2, tm)
    cand = _candidates(t1, t1_hi, t2, t2_arr, lambda a, b: a + b)
    n_c = cand.shape[0]
    cid = lax.broadcasted_iota(jnp.int32, (n_c, tm), 0)
    cur = cand
    z = jnp.zeros((1, tm), F32)
    m0 = None
    for r in range(PEER_TOPK):
        m = jnp.max(cur, axis=0, keepdims=True)
        if r == 0:
            m0 = m
        first = jnp.min(jnp.where(cur == m, cid, n_c), axis=0, keepdims=True)
        cur = jnp.where(cid == first, -jnp.inf, cur)
        z = z + jnp.exp(m - m0)
    sel = jnp.where(cur == -jnp.inf, 1.0, 0.0)
    width = [jnp.sum(sel[0:16], axis=0, keepdims=True)]
    for a in range(1, 8):
        width.append(jnp.sum(sel[8 + 8 * a:16 + 8 * a], axis=0, keepdims=True))
    for a in range(8, PEER_TOPK):
        width.append(sel[64 + a:65 + a])
    lim = jnp.zeros_like(s1)
    for a in range(PEER_TOPK):
        lim = jnp.where(rank1 == float(a), width[a], lim)
    inv_z = 1.0 / z
    e1_ref[h] = jnp.exp(s1 - t1[0]) * inv_z
    lim_ref[h] = lim
    e2_ref[h] = jnp.exp(s2 - t2[0]).astype(BF16)
    r2_ref[h] = rank2.astype(BF16)


def _peer_kernel(sc_ref, h2_ref, u_ref, vt_ref, x1_ref, mod_ref, fg_ref, o_ref,
                 e1_ref, lim_ref, e2_ref, r2_ref, acc_ref, a0_ref, a1_ref, y_ref, *, n_blk):
    s = pl.program_id(1)
    n_heads = e1_ref.shape[0]
    eb = u_ref.shape[0]
    rows_per_blk = eb // N_KEYS
    kc = 2 * N_KEYS

    def first_matmul(aw_ref):
        aw_ref[...] = _dot(u_ref[...], h2_ref[...])

    def gate_and_second_matmul(ar_ref):
        tm = ar_ref.shape[1]
        for ii in range(rows_per_blk):
            i_key = (s - 1) * rows_per_blk + ii
            g = jnp.zeros((N_KEYS, tm), BF16)
            for h in range(n_heads):
                p = e1_ref[h, pl.ds(i_key, 1), :].astype(BF16) * e2_ref[h]
                lim = lim_ref[h, pl.ds(i_key, 1), :].astype(BF16)
                g = g + jnp.where(r2_ref[h] < lim, p, jnp.zeros_like(p))
            a = ar_ref[ii * N_KEYS:(ii + 1) * N_KEYS, :]
            act = 0.5 * a * (1.0 + lax.erf(a * (0.5 ** 0.5)))
            y_ref[ii * N_KEYS:(ii + 1) * N_KEYS, :] = act.astype(BF16) * g
            if (ii + 1) * N_KEYS % kc == 0:
                c0 = (ii + 1) * N_KEYS - kc
                acc_ref[...] += _dot(vt_ref[:, c0:c0 + kc], y_ref[c0:c0 + kc, :])

    @pl.when(s == 0)
    def _():
        def head(h, carry):
            _peer_stats(sc_ref, e1_ref, lim_ref, e2_ref, r2_ref, h)
            return carry
        lax.fori_loop(0, n_heads, head, 0)
        acc_ref[...] = jnp.zeros_like(acc_ref)
        first_matmul(a0_ref)

    @pl.when((s % 2 == 1) & (s < n_blk))
    def _():
        first_matmul(a1_ref)
        gate_and_second_matmul(a0_ref)

    @pl.when((s % 2 == 0) & (s > 0) & (s < n_blk))
    def _():
        first_matmul(a0_ref)
        gate_and_second_matmul(a1_ref)

    @pl.when(s == n_blk)
    def _():
        gate_and_second_matmul(a1_ref if n_blk % 2 == 0 else a0_ref)
        peer = acc_ref[...].T
        x2 = x1_ref[...] + mod_ref[0, 5:6, :] * peer
        ms = jnp.mean(x2 * x2, axis=-1, keepdims=True)
        o_ref[...] = x2 * lax.rsqrt(ms + EPS) * fg_ref[...]


def _peer_call(sc, h2, u_bf, vt_bf, x1, mod3, final_g, seq):
    t, d = x1.shape
    tm, eb = PEER_TM, PEER_EB
    tps = seq // tm
    n_hp = sc.shape[0]
    n_exp = u_bf.shape[0]
    row = lambda i, j: (i, 0)
    n_blk = n_exp // eb
    return pl.pallas_call(
        functools.partial(_peer_kernel, n_blk=n_blk),
        grid=(t // tm, n_blk + 1),
        in_specs=[pl.BlockSpec((n_hp, N_KEYS, tm), lambda i, j: (0, 0, i)),
                  pl.BlockSpec((d, tm), lambda i, j: (0, i)),
                  pl.BlockSpec((eb, d), lambda i, j: (jnp.minimum(j, n_blk - 1), 0)),
                  pl.BlockSpec((d, eb), lambda i, j: (0, jnp.maximum(j - 1, 0))),
                  pl.BlockSpec((tm, d), row),
                  pl.BlockSpec((1, N_MOD, d), lambda i, j: (i // tps, 0, 0)),
                  pl.BlockSpec((1, d), lambda i, j: (0, 0))],
        out_specs=pl.BlockSpec((tm, d), row),
        out_shape=jax.ShapeDtypeStruct((t, d), F32),
        scratch_shapes=[pltpu.VMEM((PEER_HEADS, N_KEYS, tm), F32),
                        pltpu.VMEM((PEER_HEADS, N_KEYS, tm), F32),
                        pltpu.VMEM((PEER_HEADS, N_KEYS, tm), BF16),
                        pltpu.VMEM((PEER_HEADS, N_KEYS, tm), BF16),
                        pltpu.VMEM((d, tm), F32),
                        pltpu.VMEM((eb, tm), F32),
                        pltpu.VMEM((eb, tm), F32),
                        pltpu.VMEM((eb, tm), BF16)],
        compiler_params=_params(("arbitrary", "arbitrary")),
        name="peer",
    )(sc, h2, u_bf, vt_bf, x1, mod3, final_g)


def kernel(x, c, w_mod, b_mod, norm1_g, w_in, b_f, w_pool, pool_scale, w_out, norm2_g,
           peer_w_query, peer_sub_keys, peer_u, peer_v, final_g):
    bsz, seq, d = x.shape
    depth = w_mod.shape[0]
    t = bsz * seq
    assert d == ATTN_HEADS * HEAD_DIM and seq % INPROJ_TM == 0 and t % PEER_TM == 0
    x2d = x.reshape(t, d)
    out = x2d
    for l in range(depth):
        mod3 = _mod_call(c, w_mod[l], b_mod[l]).reshape(bsz, N_MOD, d)

        w = w_in[l]
        o_q, o_k, o_v = POOL_W, POOL_W + d, POOL_W + 2 * d
        o_f = POOL_W + 3 * d
        o_g = o_f + ATTN_HEADS
        w_tok = jnp.concatenate([w[:, :o_q], w[:, o_k:o_v], w[:, o_g:]], axis=1).astype(BF16)
        w_feat_t = jnp.concatenate([w[:, o_q:o_k], w[:, o_v:o_f]], axis=1).T.astype(BF16)
        w_f = jnp.pad(w[:, o_f:o_g], ((0, 0), (0, LANES - ATTN_HEADS))).astype(BF16)
        b_f_row = jnp.pad(b_f[l], (0, LANES - ATTN_HEADS)).reshape(1, LANES)
        u, ka, ga, gb, qt, vt = _inproj_call(
            x2d, mod3, norm1_g[l].reshape(1, d), w_tok, w_feat_t, w_f, b_f_row,
            _forget_selectors(), seq)

        yb = _attn_call(qt, ka, vt, bsz, seq)

        n_hp = 2 * PEER_HEADS
        x1, h2, sc = _mix_call(
            u, ga, gb, yb, x2d, mod3, w_pool[l].astype(BF16), pool_scale[l].reshape(1, d),
            w_out[l].astype(BF16), norm2_g[l].reshape(1, d), peer_w_query[l].astype(BF16),
            peer_sub_keys[l].reshape(n_hp, N_KEYS, -1).astype(BF16), seq)

        assert l == depth - 1 == 0
        out = _peer_call(sc, h2, peer_u[l].astype(BF16), peer_v[l].T.astype(BF16),
                         x1, mod3, final_g.reshape(1, d), seq)
        x2d = out
    return out.reshape(bsz, seq, d)
```

```python
import functools

import jax
import jax.numpy as jnp
from jax import lax
from jax.experimental import pallas as pl
from jax.experimental.pallas import tpu as pltpu

F32 = jnp.float32
BF16 = jnp.bfloat16

EPS = 1e-6
NEG_INF = -1e30

POOL_WINDOWS = (2, 4, 8, 16)
POOL_GROUP_W = 128
POOL_W = len(POOL_WINDOWS) * POOL_GROUP_W
ATTN_HEADS = 16
HEAD_DIM = 64
PEER_HEADS = 8
N_KEYS = 128
PEER_TOPK = 16
N_MOD = 6

LANES = 128
VMEM_LIMIT = 56 * 1024 * 1024

INPROJ_TM = 512
ATTN_TQ = 512
ATTN_TK = 512
MIX_TM = 256
PEER_TM = 512
PEER_EB = 2048


def _params(sem, flags=None):
    return pltpu.CompilerParams(dimension_semantics=sem, vmem_limit_bytes=VMEM_LIMIT, flags=flags)


def _split3(a):
    hi = a.astype(BF16)
    r1 = a - hi.astype(F32)
    mid = r1.astype(BF16)
    lo = (r1 - mid.astype(F32)).astype(BF16)
    return hi, mid, lo


def _dot(a, b):
    return jnp.dot(a, b, preferred_element_type=F32)


def _dot_nt(a, b):
    return lax.dot_general(a, b, (((1,), (1,)), ((), ())), preferred_element_type=F32)


def _rms_mod(x, g, shift, scale):
    ms = jnp.mean(x * x, axis=-1, keepdims=True)
    y = x * lax.rsqrt(ms + EPS) * g
    return y * (1.0 + scale) + shift


def _mod_kernel(c_ref, w_ref, b_ref, o_ref):
    c_hi, c_mid, _ = _split3(c_ref[...])
    w_hi, w_mid, _ = _split3(w_ref[...])
    acc = _dot(c_hi, w_hi) + _dot(c_hi, w_mid) + _dot(c_mid, w_hi)
    o_ref[...] = acc + b_ref[...]


def _mod_call(c, w_mod, b_mod):
    bsz, d = c.shape
    n = w_mod.shape[1]
    bn = 1024
    return pl.pallas_call(
        _mod_kernel,
        grid=(n // bn,),
        in_specs=[pl.BlockSpec((bsz, d), lambda j: (0, 0)),
                  pl.BlockSpec((d, bn), lambda j: (0, j)),
                  pl.BlockSpec((1, bn), lambda j: (0, j))],
        out_specs=pl.BlockSpec((bsz, bn), lambda j: (0, j)),
        out_shape=jax.ShapeDtypeStruct((bsz, n), F32),
        compiler_params=_params(("parallel",)),
        name="mod",
    )(c, w_mod, b_mod.reshape(1, n))


def _inproj_kernel(x_ref, mod_ref, g_ref, w_ref, wt_ref, wf_ref, bf_ref, sel_ref,
                   u_ref, ka_ref, ga_ref, gb_ref, qt_ref, vt_ref,
                   carry_ref, *, tiles_per_seq, d_model):
    i = pl.program_id(0)
    tm = x_ref.shape[0]
    h = _rms_mod(x_ref[...], g_ref[...], mod_ref[0, 0:1, :], mod_ref[0, 1:2, :])
    hb = h.astype(BF16)

    cw = 512
    n_c = d_model // cw

    def proj(c0):
        return _dot(hb, w_ref[:, c0:c0 + cw])

    u_ref[...] = proj(0).astype(BF16)
    base = POOL_W
    pairs_per_chunk = cw // LANES
    for c in range(n_c):
        kc = proj(base + c * cw).astype(BF16)
        for pp in range(pairs_per_chunk):
            p = c * pairs_per_chunk + pp
            ka_ref[:, p * 2 * LANES:p * 2 * LANES + LANES] = kc[:, pp * LANES:(pp + 1) * LANES]
    base += d_model
    for c in range(n_c):
        ga_ref[:, c * cw:(c + 1) * cw] = jax.nn.sigmoid(proj(base + c * cw)).astype(BF16)
    base += d_model
    for c in range(n_c):
        gb_ref[:, c * cw:(c + 1) * cw] = jax.nn.sigmoid(proj(base + c * cw)).astype(BF16)

    scale = HEAD_DIM ** -0.5
    for c in range(n_c):
        qt_ref[c * cw:(c + 1) * cw, :] = (_dot_nt(wt_ref[c * cw:(c + 1) * cw, :], hb) * scale).astype(BF16)
    for c in range(n_c):
        r0 = d_model + c * cw
        vt_ref[c * cw:(c + 1) * cw, :] = _dot_nt(wt_ref[r0:r0 + cw, :], hb).astype(BF16)

    zf = _dot(hb, wf_ref[...]) + bf_ref[...]
    logf = jnp.minimum(zf, 0.0) - jnp.log1p(jnp.exp(-jnp.abs(zf)))
    row = lax.broadcasted_iota(jnp.int32, (tm, tm), 0)
    col = lax.broadcasted_iota(jnp.int32, (tm, tm), 1)
    tril = jnp.where(row >= col, 1.0, 0.0).astype(BF16)
    hi, mid, lo = _split3(logf)
    cs = _dot(tril, hi) + _dot(tril, mid) + _dot(tril, lo)

    @pl.when(i % tiles_per_seq == 0)
    def _():
        carry_ref[...] = jnp.zeros_like(carry_ref)

    f_cum = cs + carry_ref[0:1, :]
    carry_ref[...] = jnp.broadcast_to(f_cum[tm - 1:tm, :], carry_ref.shape)
    n_hi, n_mid, n_lo = _split3(-f_cum)
    faug = _dot(n_hi, sel_ref[0]) + _dot(n_mid, sel_ref[1]) + _dot(n_lo, sel_ref[2])
    for p in range(d_model // LANES):
        ka_ref[:, p * 2 * LANES + LANES:(p + 1) * 2 * LANES] = faug[:, p * LANES:(p + 1) * LANES].astype(BF16)


def _inproj_call(x2d, mod3, norm_g, w_tok, w_feat_t, w_f, b_f, sel, seq):
    t, d = x2d.shape
    tm = INPROJ_TM
    tps = seq // tm
    row = lambda i: (i, 0)
    colb = lambda i: (0, i)
    c2 = lambda i: (0, 0)
    out_shape = (
        jax.ShapeDtypeStruct((t, POOL_W), BF16),
        jax.ShapeDtypeStruct((t, 2 * d), BF16),
        jax.ShapeDtypeStruct((t, d), BF16), jax.ShapeDtypeStruct((t, d), BF16),
        jax.ShapeDtypeStruct((d, t), BF16), jax.ShapeDtypeStruct((d, t), BF16),
    )
    out_specs = (
        pl.BlockSpec((tm, POOL_W), row),
        pl.BlockSpec((tm, 2 * d), row),
        pl.BlockSpec((tm, d), row), pl.BlockSpec((tm, d), row),
        pl.BlockSpec((d, tm), colb), pl.BlockSpec((d, tm), colb),
    )
    return pl.pallas_call(
        functools.partial(_inproj_kernel, tiles_per_seq=tps, d_model=d),
        grid=(t // tm,),
        in_specs=[pl.BlockSpec((tm, d), row),
                  pl.BlockSpec((1, N_MOD, d), lambda i: (i // tps, 0, 0)),
                  pl.BlockSpec((1, d), c2),
                  pl.BlockSpec(w_tok.shape, c2),
                  pl.BlockSpec(w_feat_t.shape, c2),
                  pl.BlockSpec(w_f.shape, c2),
                  pl.BlockSpec(b_f.shape, c2),
                  pl.BlockSpec(sel.shape, lambda i: (0, 0, 0))],
        out_specs=out_specs,
        out_shape=out_shape,
        scratch_shapes=[pltpu.VMEM((8, LANES), F32)],
        compiler_params=_params(("arbitrary",)),
        name="inproj",
    )(x2d, mod3, norm_g, w_tok, w_feat_t, w_f, b_f, sel)


def _forget_selectors():
    h = jnp.arange(LANES)[:, None]
    c = jnp.arange(ATTN_HEADS // 2 * LANES)[None, :]
    sels = []
    for r in range(3):
        hit = (h < ATTN_HEADS) & (c == (h // 2) * LANES + 3 * (h % 2) + r)
        sels.append(jnp.where(hit, 1.0, 0.0))
    return jnp.stack(sels).astype(BF16)


def _attn_kernel(qt_ref, ka_ref, vt_ref, o_ref, *, tq, tk):
    seq = ka_ref.shape[0]
    rowid = lax.broadcasted_iota(jnp.int32, (LANES, tq), 0)
    krow = lax.broadcasted_iota(jnp.int32, (tk, tq), 0)
    qcol = lax.broadcasted_iota(jnp.int32, (tk, tq), 1)
    aug = [jnp.where((rowid >= 3 * h) & (rowid < 3 * h + 3), 1.0, 0.0).astype(BF16) for h in range(2)]
    head = [jnp.where(rowid // HEAD_DIM == h, 1.0, 0.0).astype(BF16) for h in range(2)]

    def q_tile(qi, carry):
        q0 = pl.multiple_of(qi * tq, tq)
        qt = qt_ref[:, pl.ds(q0, tq)]
        ws = [jnp.concatenate([qt * head[h], aug[h]], axis=0) for h in range(2)]

        def kv_step(j, state, diag):
            k0 = pl.multiple_of(j * tk, tk)
            ka = ka_ref[pl.ds(k0, tk), :]
            new = []
            for h in range(2):
                m, l, acc = state[h]
                s = _dot(ka, ws[h])
                if diag is not None:
                    s = jnp.where(krow + diag * tk <= qcol, s, NEG_INF)
                m_new = jnp.maximum(m, jnp.max(s, axis=0, keepdims=True))
                alpha = jnp.exp(m - m_new)
                p = jnp.exp(s - m_new)
                l = alpha * l + jnp.sum(p, axis=0, keepdims=True)
                vt = vt_ref[h * HEAD_DIM:(h + 1) * HEAD_DIM, pl.ds(k0, tk)]
                acc = alpha * acc + _dot(vt, p.astype(BF16))
                new.append((m_new, l, acc))
            return tuple(new)

        init = tuple((jnp.full((1, tq), NEG_INF, F32), jnp.zeros((1, tq), F32),
                      jnp.zeros((HEAD_DIM, tq), F32)) for _ in range(2))
        n_full = qi * (tq // tk)
        state = lax.fori_loop(0, n_full, lambda j, st: kv_step(j, st, None), init)
        for dblk in range(tq // tk):
            state = kv_step(n_full + dblk, state, dblk)
        out_t = jnp.concatenate([state[0][2] / state[0][1], state[1][2] / state[1][1]], axis=0)
        o_ref[pl.ds(q0, tq), :] = out_t.T.astype(o_ref.dtype)
        return carry

    lax.fori_loop(0, seq // tq, q_tile, 0)


def _attn_call(qt, ka, vt, bsz, seq):
    d, t = qt.shape
    n_pairs = d // LANES
    return pl.pallas_call(
        functools.partial(_attn_kernel, tq=ATTN_TQ, tk=ATTN_TK),
        grid=(bsz, n_pairs),
        in_specs=[pl.BlockSpec((LANES, seq), lambda b, p: (p, b)),
                  pl.BlockSpec((seq, 2 * LANES), lambda b, p: (b, p)),
                  pl.BlockSpec((LANES, seq), lambda b, p: (p, b))],
        out_specs=pl.BlockSpec((seq, LANES), lambda b, p: (b, p)),
        out_shape=jax.ShapeDtypeStruct((t, d), BF16),
        compiler_params=_params(("parallel", "parallel")),
        name="attn",
    )(qt, ka, vt)


def _mix_kernel(uc_ref, up_ref, ga_ref, gb_ref, yb_ref, x_ref, mod_ref, wp_ref, ps_ref,
                wo_ref, g2_ref, wq_ref, sk_ref,
                x1_ref, h2_ref, sc_ref, y_ref, *, tiles_per_seq):
    i = pl.program_id(0)
    tm = x_ref.shape[0]
    halo = LANES
    first = (i % tiles_per_seq) == 0
    pos0 = (i % tiles_per_seq) * tm
    r_d = lax.broadcasted_iota(jnp.int32, (tm, tm), 0)
    c_d = lax.broadcasted_iota(jnp.int32, (tm, tm), 1)
    r_o = lax.broadcasted_iota(jnp.int32, (tm, halo), 0)
    c_o = lax.broadcasted_iota(jnp.int32, (tm, halo), 1)
    pos = pos0 + lax.broadcasted_iota(jnp.int32, (tm, 1), 0)
    gw = wp_ref.shape[2]
    for g, w in enumerate(POOL_WINDOWS):
        lo, hi = g * POOL_GROUP_W, (g + 1) * POOL_GROUP_W
        u_cur = uc_ref[:, lo:hi]
        u_prev = up_ref[tm - halo:tm, lo:hi]
        lag = r_d - c_d
        band_d = jnp.where((lag >= 0) & (lag < w), 1.0, 0.0).astype(BF16)
        band_o = jnp.where((r_o + halo - c_o < w) & jnp.logical_not(first), 1.0, 0.0).astype(BF16)
        wsum = _dot(band_d, u_cur) + _dot(band_o, u_prev)
        cnt = jnp.minimum(pos + 1, w).astype(F32)
        pooled = wsum / cnt - u_cur.astype(F32)
        ya = _dot(pooled.astype(BF16), wp_ref[g]) * ps_ref[:, g * gw:(g + 1) * gw]
        sl = slice(g * gw, (g + 1) * gw)
        y = ga_ref[:, sl].astype(F32) * ya + gb_ref[:, sl].astype(F32) * yb_ref[:, sl].astype(F32)
        y_ref[:, sl] = y.astype(BF16)

    o = _dot(y_ref[...], wo_ref[...])
    x1 = x_ref[...] + mod_ref[0, 2:3, :] * o
    x1_ref[...] = x1
    h2f = _rms_mod(x1, g2_ref[...], mod_ref[0, 3:4, :], mod_ref[0, 4:5, :])
    h2 = h2f.astype(BF16)
    h2_ref[...] = h2f.T.astype(BF16)
    n_hp = sk_ref.shape[0]
    half = sk_ref.shape[2]
    qw = 512
    per = qw // half
    for c in range(n_hp // per):
        qp = _dot(h2, wq_ref[:, c * qw:(c + 1) * qw]).astype(BF16)
        for k in range(per):
            hp = c * per + k
            sc_ref[hp] = _dot_nt(sk_ref[hp], qp[:, k * half:(k + 1) * half])


def _mix_call(u, ga, gb, yb, x2d, mod3, w_pool, pool_scale, w_out, norm2_g, w_query, sub_keys, seq):
    t, d = x2d.shape
    tm = MIX_TM
    tps = seq // tm
    row = lambda i: (i, 0)
    c2 = lambda i: (0, 0)
    c3 = lambda i: (0, 0, 0)
    n_hp = sub_keys.shape[0]
    return pl.pallas_call(
        functools.partial(_mix_kernel, tiles_per_seq=tps),
        grid=(t // tm,),
        in_specs=[pl.BlockSpec((tm, POOL_W), row),
                  pl.BlockSpec((tm, POOL_W), lambda i: (jnp.maximum(i - 1, 0), 0)),
                  pl.BlockSpec((tm, d), row), pl.BlockSpec((tm, d), row), pl.BlockSpec((tm, d), row),
                  pl.BlockSpec((tm, d), row),
                  pl.BlockSpec((1, N_MOD, d), lambda i: (i // tps, 0, 0)),
                  pl.BlockSpec(w_pool.shape, c3),
                  pl.BlockSpec((1, d), c2),
                  pl.BlockSpec((d, d), c2),
                  pl.BlockSpec((1, d), c2),
                  pl.BlockSpec(w_query.shape, c2),
                  pl.BlockSpec(sub_keys.shape, c3)],
        out_specs=(pl.BlockSpec((tm, d), row), pl.BlockSpec((d, tm), lambda i: (0, i)),
                   pl.BlockSpec((n_hp, N_KEYS, tm), lambda i: (0, 0, i))),
        out_shape=(jax.ShapeDtypeStruct((t, d), F32), jax.ShapeDtypeStruct((d, t), BF16),
                   jax.ShapeDtypeStruct((n_hp, N_KEYS, t), F32)),
        scratch_shapes=[pltpu.VMEM((tm, d), BF16)],
        compiler_params=_params(("parallel",)),
        name="mix",
    )(u, u, ga, gb, yb, x2d, mod3, w_pool, pool_scale, w_out, norm2_g, w_query, sub_keys)


RANK_CODE_BASE = -(2.0 ** 127)


def _top16_rows(s):
    cur = s
    rows = []
    for r in range(PEER_TOPK):
        m = jnp.max(cur, axis=0, keepdims=True)
        rows.append(m)
        cur = jnp.where(cur == m, RANK_CODE_BASE * (1.0 + r / 64.0), cur)
    bits = lax.bitcast_convert_type(cur, jnp.int32)
    coded = ((bits >> 17) & 63).astype(F32)
    rank = jnp.where(cur <= RANK_CODE_BASE, coded, float(PEER_TOPK))
    return rows, rank


def _stack_rows(rows, tm):
    n = len(rows)
    rid = lax.broadcasted_iota(jnp.int32, (n, tm), 0)
    arr = jnp.zeros((n, tm), F32)
    for r, v in enumerate(rows):
        arr = jnp.where(rid == r, v, arr)
    return arr


def _candidates(rows1, arr1_hi, rows2, arr2, combine):
    pieces = [combine(rows1[0], arr2)]
    for a in range(1, 8):
        pieces.append(combine(rows1[a], arr2[0:8]))
    pieces.append(combine(arr1_hi, rows2[0]))
    return jnp.concatenate(pieces, axis=0)


def _peer_stats(sc_ref, e1_ref, lim_ref, e2_ref, r2_ref, h):
    tm = sc_ref.shape[2]
    s1 = sc_ref[2 * h]
    s2 = sc_ref[2 * h + 1]
    t1, rank1 = _top16_rows(s1)
    t2, rank2 = _top16_rows(s2)
    t1_hi = _stack_rows(t1[8:], tm)
    t2_arr = _stack_rows(t2, tm)
    cand = _candidates(t1, t1_hi, t2, t2_arr, lambda a, b: a + b)
    n_c = cand.shape[0]
    cid = lax.broadcasted_iota(jnp.int32, (n_c, tm), 0)
    cur = cand
    z = jnp.zeros((1, tm), F32)
    m0 = None
    for r in range(PEER_TOPK):
        m = jnp.max(cur, axis=0, keepdims=True)
        if r == 0:
            m0 = m
        first = jnp.min(jnp.where(cur == m, cid, n_c), axis=0, keepdims=True)
        cur = jnp.where(cid == first, -jnp.inf, cur)
        z = z + jnp.exp(m - m0)
    sel = jnp.where(cur == -jnp.inf, 1.0, 0.0)
    width = [jnp.sum(sel[0:16], axis=0, keepdims=True)]
    for a in range(1, 8):
        width.append(jnp.sum(sel[8 + 8 * a:16 + 8 * a], axis=0, keepdims=True))
    for a in range(8, PEER_TOPK):
        width.append(sel[64 + a:65 + a])
    lim = jnp.zeros_like(s1)
    for a in range(PEER_TOPK):
        lim = jnp.where(rank1 == float(a), width[a], lim)
    inv_z = 1.0 / z
    e1_ref[h] = jnp.exp(s1 - t1[0]) * inv_z
    lim_ref[h] = lim
    e2_ref[h] = jnp.exp(s2 - t2[0]).astype(BF16)
    r2_ref[h] = rank2.astype(BF16)


def _peer_kernel(sc_ref, h2_ref, u_ref, vt_ref, x1_ref, mod_ref, fg_ref, o_ref,
                 e1_ref, lim_ref, e2_ref, r2_ref, acc_ref, a0_ref, a1_ref, y_ref, *, n_blk):
    s = pl.program_id(1)
    n_heads = e1_ref.shape[0]
    eb = u_ref.shape[0]
    rows_per_blk = eb // N_KEYS
    kc = 2 * N_KEYS

    def first_matmul(aw_ref):
        aw_ref[...] = _dot(u_ref[...], h2_ref[...])

    def gate_and_second_matmul(ar_ref):
        tm = ar_ref.shape[1]
        for ii in range(rows_per_blk):
            i_key = (s - 1) * rows_per_blk + ii
            g = jnp.zeros((N_KEYS, tm), BF16)
            for h in range(n_heads):
                p = e1_ref[h, pl.ds(i_key, 1), :].astype(BF16) * e2_ref[h]
                lim = lim_ref[h, pl.ds(i_key, 1), :].astype(BF16)
                g = g + jnp.where(r2_ref[h] < lim, p, jnp.zeros_like(p))
            a = ar_ref[ii * N_KEYS:(ii + 1) * N_KEYS, :]
            act = 0.5 * a * (1.0 + lax.erf(a * (0.5 ** 0.5)))
            y_ref[ii * N_KEYS:(ii + 1) * N_KEYS, :] = act.astype(BF16) * g
            if (ii + 1) * N_KEYS % kc == 0:
                c0 = (ii + 1) * N_KEYS - kc
                acc_ref[...] += _dot(vt_ref[:, c0:c0 + kc], y_ref[c0:c0 + kc, :])

    @pl.when(s == 0)
    def _():
        def head(h, carry):
            _peer_stats(sc_ref, e1_ref, lim_ref, e2_ref, r2_ref, h)
            return carry
        lax.fori_loop(0, n_heads, head, 0)
        acc_ref[...] = jnp.zeros_like(acc_ref)
        first_matmul(a0_ref)

    @pl.when((s % 2 == 1) & (s < n_blk))
    def _():
        first_matmul(a1_ref)
        gate_and_second_matmul(a0_ref)

    @pl.when((s % 2 == 0) & (s > 0) & (s < n_blk))
    def _():
        first_matmul(a0_ref)
        gate_and_second_matmul(a1_ref)

    @pl.when(s == n_blk)
    def _():
        gate_and_second_matmul(a1_ref if n_blk % 2 == 0 else a0_ref)
        peer = acc_ref[...].T
        x2 = x1_ref[...] + mod_ref[0, 5:6, :] * peer
        ms = jnp.mean(x2 * x2, axis=-1, keepdims=True)
        o_ref[...] = x2 * lax.rsqrt(ms + EPS) * fg_ref[...]


def _peer_call(sc, h2, u_bf, vt_bf, x1, mod3, final_g, seq):
    t, d = x1.shape
    tm, eb = PEER_TM, PEER_EB
    tps = seq // tm
    n_hp = sc.shape[0]
    n_exp = u_bf.shape[0]
    row = lambda i, j: (i, 0)
    n_blk = n_exp // eb
    return pl.pallas_call(
        functools.partial(_peer_kernel, n_blk=n_blk),
        grid=(t // tm, n_blk + 1),
        in_specs=[pl.BlockSpec((n_hp, N_KEYS, tm), lambda i, j: (0, 0, i)),
                  pl.BlockSpec((d, tm), lambda i, j: (0, i)),
                  pl.BlockSpec((eb, d), lambda i, j: (jnp.minimum(j, n_blk - 1), 0)),
                  pl.BlockSpec((d, eb), lambda i, j: (0, jnp.maximum(j - 1, 0))),
                  pl.BlockSpec((tm, d), row),
                  pl.BlockSpec((1, N_MOD, d), lambda i, j: (i // tps, 0, 0)),
                  pl.BlockSpec((1, d), lambda i, j: (0, 0))],
        out_specs=pl.BlockSpec((tm, d), row),
        out_shape=jax.ShapeDtypeStruct((t, d), F32),
        scratch_shapes=[pltpu.VMEM((PEER_HEADS, N_KEYS, tm), F32),
                        pltpu.VMEM((PEER_HEADS, N_KEYS, tm), F32),
                        pltpu.VMEM((PEER_HEADS, N_KEYS, tm), BF16),
                        pltpu.VMEM((PEER_HEADS, N_KEYS, tm), BF16),
                        pltpu.VMEM((d, tm), F32),
                        pltpu.VMEM((eb, tm), F32),
                        pltpu.VMEM((eb, tm), F32),
                        pltpu.VMEM((eb, tm), BF16)],
        compiler_params=_params(("arbitrary", "arbitrary")),
        name="peer",
    )(sc, h2, u_bf, vt_bf, x1, mod3, final_g)


def kernel(x, c, w_mod, b_mod, norm1_g, w_in, b_f, w_pool, pool_scale, w_out, norm2_g,
           peer_w_query, peer_sub_keys, peer_u, peer_v, final_g):
    bsz, seq, d = x.shape
    depth = w_mod.shape[0]
    t = bsz * seq
    assert d == ATTN_HEADS * HEAD_DIM and seq % INPROJ_TM == 0 and t % PEER_TM == 0
    x2d = x.reshape(t, d)
    out = x2d
    for l in range(depth):
        mod3 = _mod_call(c, w_mod[l], b_mod[l]).reshape(bsz, N_MOD, d)

        w = w_in[l]
        o_q, o_k, o_v = POOL_W, POOL_W + d, POOL_W + 2 * d
        o_f = POOL_W + 3 * d
        o_g = o_f + ATTN_HEADS
        w_tok = jnp.concatenate([w[:, :o_q], w[:, o_k:o_v], w[:, o_g:]], axis=1).astype(BF16)
        w_feat_t = jnp.concatenate([w[:, o_q:o_k], w[:, o_v:o_f]], axis=1).T.astype(BF16)
        w_f = jnp.pad(w[:, o_f:o_g], ((0, 0), (0, LANES - ATTN_HEADS))).astype(BF16)
        b_f_row = jnp.pad(b_f[l], (0, LANES - ATTN_HEADS)).reshape(1, LANES)
        u, ka, ga, gb, qt, vt = _inproj_call(
            x2d, mod3, norm1_g[l].reshape(1, d), w_tok, w_feat_t, w_f, b_f_row,
            _forget_selectors(), seq)

        yb = _attn_call(qt, ka, vt, bsz, seq)

        n_hp = 2 * PEER_HEADS
        x1, h2, sc = _mix_call(
            u, ga, gb, yb, x2d, mod3, w_pool[l].astype(BF16), pool_scale[l].reshape(1, d),
            w_out[l].astype(BF16), norm2_g[l].reshape(1, d), peer_w_query[l].astype(BF16),
            peer_sub_keys[l].reshape(n_hp, N_KEYS, -1).astype(BF16), seq)

        assert l == depth - 1 == 0
        out = _peer_call(sc, h2, peer_u[l].astype(BF16), peer_v[l].T.astype(BF16),
                         x1, mod3, final_g.reshape(1, d), seq)
        x2d = out
    return out.reshape(bsz, seq, d)
```

```python
import functools

import jax
import jax.numpy as jnp
from jax import lax
from jax.experimental import pallas as pl
from jax.experimental.pallas import tpu as pltpu

F32 = jnp.float32
BF16 = jnp.bfloat16

EPS = 1e-6
NEG_INF = -1e30
LOG2_E = 1.4426950408889634

POOL_WINDOWS = (2, 4, 8, 16)
POOL_GROUP_W = 128
POOL_W = len(POOL_WINDOWS) * POOL_GROUP_W
ATTN_HEADS = 16
HEAD_DIM = 64
PEER_HEADS = 8
N_KEYS = 128
PEER_TOPK = 16
N_MOD = 6

LANES = 128
VMEM_LIMIT = 56 * 1024 * 1024

INPROJ_TM = 512
ATTN_TQ = 512
ATTN_TK = 512
MIX_TM = 256
PEER_TM = 512
PEER_EB = 2048


def _params(sem, flags=None):
    return pltpu.CompilerParams(dimension_semantics=sem, vmem_limit_bytes=VMEM_LIMIT, flags=flags)


def _split3(a):
    hi = a.astype(BF16)
    r1 = a - hi.astype(F32)
    mid = r1.astype(BF16)
    lo = (r1 - mid.astype(F32)).astype(BF16)
    return hi, mid, lo


def _dot(a, b):
    return jnp.dot(a, b, preferred_element_type=F32)


def _dot_nt(a, b):
    return lax.dot_general(a, b, (((1,), (1,)), ((), ())), preferred_element_type=F32)


def _rms_mod(x, g, shift, scale):
    ms = jnp.mean(x * x, axis=-1, keepdims=True)
    y = x * lax.rsqrt(ms + EPS) * g
    return y * (1.0 + scale) + shift


def _mod_kernel(c_ref, w_ref, b_ref, o_ref):
    c_hi, c_mid, _ = _split3(c_ref[...])
    w_hi, w_mid, _ = _split3(w_ref[...])
    acc = _dot(c_hi, w_hi) + _dot(c_hi, w_mid) + _dot(c_mid, w_hi)
    o_ref[...] = acc + b_ref[...]


def _mod_call(c, w_mod, b_mod):
    bsz, d = c.shape
    n = w_mod.shape[1]
    bn = 1024
    return pl.pallas_call(
        _mod_kernel,
        grid=(n // bn,),
        in_specs=[pl.BlockSpec((bsz, d), lambda j: (0, 0)),
                  pl.BlockSpec((d, bn), lambda j: (0, j)),
                  pl.BlockSpec((1, bn), lambda j: (0, j))],
        out_specs=pl.BlockSpec((bsz, bn), lambda j: (0, j)),
        out_shape=jax.ShapeDtypeStruct((bsz, n), F32),
        compiler_params=_params(("parallel",)),
        name="mod",
    )(c, w_mod, b_mod.reshape(1, n))


def _inproj_kernel(x_ref, mod_ref, g_ref, w_ref, wt_ref, wf_ref, bf_ref, sel_ref,
                   u_ref, ka_ref, ga_ref, gb_ref, qt_ref, vt_ref,
                   carry_ref, *, tiles_per_seq, d_model):
    i = pl.program_id(0)
    tm = x_ref.shape[0]
    h = _rms_mod(x_ref[...], g_ref[...], mod_ref[0, 0:1, :], mod_ref[0, 1:2, :])
    hb = h.astype(BF16)

    cw = 512
    n_c = d_model // cw

    def proj(c0):
        return _dot(hb, w_ref[:, c0:c0 + cw])

    u_ref[...] = proj(0).astype(BF16)
    base = POOL_W
    pairs_per_chunk = cw // LANES
    for c in range(n_c):
        kc = proj(base + c * cw).astype(BF16)
        for pp in range(pairs_per_chunk):
            p = c * pairs_per_chunk + pp
            ka_ref[:, p * 2 * LANES:p * 2 * LANES + LANES] = kc[:, pp * LANES:(pp + 1) * LANES]
    base += d_model
    for c in range(n_c):
        ga_ref[:, c * cw:(c + 1) * cw] = jax.nn.sigmoid(proj(base + c * cw)).astype(BF16)
    base += d_model
    for c in range(n_c):
        gb_ref[:, c * cw:(c + 1) * cw] = jax.nn.sigmoid(proj(base + c * cw)).astype(BF16)

    scale = HEAD_DIM ** -0.5 * LOG2_E
    for c in range(n_c):
        qt_ref[c * cw:(c + 1) * cw, :] = (_dot_nt(wt_ref[c * cw:(c + 1) * cw, :], hb) * scale).astype(BF16)
    for c in range(n_c):
        r0 = d_model + c * cw
        vt_ref[c * cw:(c + 1) * cw, :] = _dot_nt(wt_ref[r0:r0 + cw, :], hb).astype(BF16)

    zf = _dot(hb, wf_ref[...]) + bf_ref[...]
    logf = jnp.minimum(zf, 0.0) - jnp.log1p(jnp.exp(-jnp.abs(zf)))
    row = lax.broadcasted_iota(jnp.int32, (tm, tm), 0)
    col = lax.broadcasted_iota(jnp.int32, (tm, tm), 1)
    tril = jnp.where(row >= col, 1.0, 0.0).astype(BF16)
    hi, mid, lo = _split3(logf)
    cs = _dot(tril, hi) + _dot(tril, mid) + _dot(tril, lo)

    @pl.when(i % tiles_per_seq == 0)
    def _():
        carry_ref[...] = jnp.zeros_like(carry_ref)

    f_cum = cs + carry_ref[0:1, :]
    carry_ref[...] = jnp.broadcast_to(f_cum[tm - 1:tm, :], carry_ref.shape)
    n_hi, n_mid, n_lo = _split3(f_cum * -LOG2_E)
    faug = _dot(n_hi, sel_ref[0]) + _dot(n_mid, sel_ref[1]) + _dot(n_lo, sel_ref[2])
    for p in range(d_model // LANES):
        ka_ref[:, p * 2 * LANES + LANES:(p + 1) * 2 * LANES] = faug[:, p * LANES:(p + 1) * LANES].astype(BF16)


def _inproj_call(x2d, mod3, norm_g, w_tok, w_feat_t, w_f, b_f, sel, seq):
    t, d = x2d.shape
    tm = INPROJ_TM
    tps = seq // tm
    row = lambda i: (i, 0)
    colb = lambda i: (0, i)
    c2 = lambda i: (0, 0)
    out_shape = (
        jax.ShapeDtypeStruct((t, POOL_W), BF16),
        jax.ShapeDtypeStruct((t, 2 * d), BF16),
        jax.ShapeDtypeStruct((t, d), BF16), jax.ShapeDtypeStruct((t, d), BF16),
        jax.ShapeDtypeStruct((d, t), BF16), jax.ShapeDtypeStruct((d, t), BF16),
    )
    out_specs = (
        pl.BlockSpec((tm, POOL_W), row),
        pl.BlockSpec((tm, 2 * d), row),
        pl.BlockSpec((tm, d), row), pl.BlockSpec((tm, d), row),
        pl.BlockSpec((d, tm), colb), pl.BlockSpec((d, tm), colb),
    )
    return pl.pallas_call(
        functools.partial(_inproj_kernel, tiles_per_seq=tps, d_model=d),
        grid=(t // tm,),
        in_specs=[pl.BlockSpec((tm, d), row),
                  pl.BlockSpec((1, N_MOD, d), lambda i: (i // tps, 0, 0)),
                  pl.BlockSpec((1, d), c2),
                  pl.BlockSpec(w_tok.shape, c2),
                  pl.BlockSpec(w_feat_t.shape, c2),
                  pl.BlockSpec(w_f.shape, c2),
                  pl.BlockSpec(b_f.shape, c2),
                  pl.BlockSpec(sel.shape, lambda i: (0, 0, 0))],
        out_specs=out_specs,
        out_shape=out_shape,
        scratch_shapes=[pltpu.VMEM((8, LANES), F32)],
        compiler_params=_params(("arbitrary",)),
        name="inproj",
    )(x2d, mod3, norm_g, w_tok, w_feat_t, w_f, b_f, sel)


def _forget_selectors():
    h = jnp.arange(LANES)[:, None]
    c = jnp.arange(ATTN_HEADS // 2 * LANES)[None, :]
    sels = []
    for r in range(3):
        hit = (h < ATTN_HEADS) & (c == (h // 2) * LANES + 3 * (h % 2) + r)
        sels.append(jnp.where(hit, 1.0, 0.0))
    return jnp.stack(sels).astype(BF16)


SUM_ROWS = 16


def _attn_kernel(qt_ref, ka_ref, vt_ref, o_ref, *, tq, tk):
    seq = ka_ref.shape[0]
    rowid = lax.broadcasted_iota(jnp.int32, (LANES, tq), 0)
    krow = lax.broadcasted_iota(jnp.int32, (tk, tq), 0)
    qcol = lax.broadcasted_iota(jnp.int32, (tk, tq), 1)
    aug = [jnp.where((rowid >= 3 * h) & (rowid < 3 * h + 3), 1.0, 0.0).astype(BF16) for h in range(2)]
    head = [jnp.where(rowid // HEAD_DIM == h, 1.0, 0.0).astype(BF16) for h in range(2)]
    ones_rows = jnp.ones((SUM_ROWS, tk), BF16)

    def q_tile(qi, carry):
        q0 = pl.multiple_of(qi * tq, tq)
        qt = qt_ref[:, pl.ds(q0, tq)]
        ws = [jnp.concatenate([qt * head[h], aug[h]], axis=0) for h in range(2)]

        def qk(j):
            k0 = pl.multiple_of(j * tk, tk)
            ka = ka_ref[pl.ds(k0, tk), :]
            return tuple(_dot(ka, ws[h]) for h in range(2))

        def kv_step(j, state, scores, diag):
            k0 = pl.multiple_of(j * tk, tk)
            new = []
            for h in range(2):
                m, acc = state[h]
                s = scores[h]
                if diag is not None:
                    s = jnp.where(krow + diag * tk <= qcol, s, NEG_INF)
                m_new = jnp.maximum(m, jnp.max(s, axis=0, keepdims=True))
                alpha = jnp.exp2(m - m_new)
                p = jnp.exp2(s - m_new)
                vt = vt_ref[h * HEAD_DIM:(h + 1) * HEAD_DIM, pl.ds(k0, tk)]
                acc = alpha * acc + _dot(jnp.concatenate([vt, ones_rows], axis=0), p.astype(BF16))
                new.append((m_new, acc))
            return tuple(new)

        init = tuple((jnp.full((1, tq), NEG_INF, F32), jnp.zeros((HEAD_DIM + SUM_ROWS, tq), F32))
                     for _ in range(2))
        state = lax.fori_loop(0, qi, lambda j, st: kv_step(j, st, qk(j), None), init)
        state = kv_step(qi, state, qk(qi), 0)
        out_t = jnp.concatenate([st[1][:HEAD_DIM] / st[1][HEAD_DIM:HEAD_DIM + 1] for st in state], axis=0)
        o_ref[pl.ds(q0, tq), :] = out_t.T.astype(o_ref.dtype)
        return carry

    lax.fori_loop(0, seq // tq, q_tile, 0)


def _attn_call(qt, ka, vt, bsz, seq):
    d, t = qt.shape
    n_pairs = d // LANES
    return pl.pallas_call(
        functools.partial(_attn_kernel, tq=ATTN_TQ, tk=ATTN_TK),
        grid=(bsz, n_pairs),
        in_specs=[pl.BlockSpec((LANES, seq), lambda b, p: (p, b)),
                  pl.BlockSpec((seq, 2 * LANES), lambda b, p: (b, p)),
                  pl.BlockSpec((LANES, seq), lambda b, p: (p, b))],
        out_specs=pl.BlockSpec((seq, LANES), lambda b, p: (b, p)),
        out_shape=jax.ShapeDtypeStruct((t, d), BF16),
        compiler_params=_params(("parallel", "parallel")),
        name="attn",
    )(qt, ka, vt)


def _mix_kernel(uc_ref, up_ref, ga_ref, gb_ref, yb_ref, x_ref, mod_ref, wp_ref, ps_ref,
                wo_ref, g2_ref, wq_ref, sk_ref,
                x1_ref, h2_ref, sc_ref, y_ref, *, tiles_per_seq):
    i = pl.program_id(0)
    tm = x_ref.shape[0]
    halo = LANES
    first = (i % tiles_per_seq) == 0
    pos0 = (i % tiles_per_seq) * tm
    r_d = lax.broadcasted_iota(jnp.int32, (tm, tm), 0)
    c_d = lax.broadcasted_iota(jnp.int32, (tm, tm), 1)
    r_o = lax.broadcasted_iota(jnp.int32, (tm, halo), 0)
    c_o = lax.broadcasted_iota(jnp.int32, (tm, halo), 1)
    pos = pos0 + lax.broadcasted_iota(jnp.int32, (tm, 1), 0)
    gw = wp_ref.shape[2]
    for g, w in enumerate(POOL_WINDOWS):
        lo, hi = g * POOL_GROUP_W, (g + 1) * POOL_GROUP_W
        u_cur = uc_ref[:, lo:hi]
        u_prev = up_ref[tm - halo:tm, lo:hi]
        lag = r_d - c_d
        band_d = jnp.where((lag >= 0) & (lag < w), 1.0, 0.0).astype(BF16)
        band_o = jnp.where((r_o + halo - c_o < w) & jnp.logical_not(first), 1.0, 0.0).astype(BF16)
        wsum = _dot(band_d, u_cur) + _dot(band_o, u_prev)
        cnt = jnp.minimum(pos + 1, w).astype(F32)
        pooled = wsum / cnt - u_cur.astype(F32)
        ya = _dot(pooled.astype(BF16), wp_ref[g]) * ps_ref[:, g * gw:(g + 1) * gw]
        sl = slice(g * gw, (g + 1) * gw)
        y = ga_ref[:, sl].astype(F32) * ya + gb_ref[:, sl].astype(F32) * yb_ref[:, sl].astype(F32)
        y_ref[:, sl] = y.astype(BF16)

    o = _dot(y_ref[...], wo_ref[...])
    x1 = x_ref[...] + mod_ref[0, 2:3, :] * o
    x1_ref[...] = x1
    h2f = _rms_mod(x1, g2_ref[...], mod_ref[0, 3:4, :], mod_ref[0, 4:5, :])
    h2 = h2f.astype(BF16)
    h2_ref[...] = h2f.T.astype(BF16)
    n_hp = sk_ref.shape[0]
    half = sk_ref.shape[2]
    qw = 512
    per = qw // half
    for c in range(n_hp // per):
        qp = _dot(h2, wq_ref[:, c * qw:(c + 1) * qw]).astype(BF16)
        for k in range(per):
            hp = c * per + k
            sc_ref[hp] = _dot_nt(sk_ref[hp], qp[:, k * half:(k + 1) * half])


def _mix_call(u, ga, gb, yb, x2d, mod3, w_pool, pool_scale, w_out, norm2_g, w_query, sub_keys, seq):
    t, d = x2d.shape
    tm = MIX_TM
    tps = seq // tm
    row = lambda i: (i, 0)
    c2 = lambda i: (0, 0)
    c3 = lambda i: (0, 0, 0)
    n_hp = sub_keys.shape[0]
    return pl.pallas_call(
        functools.partial(_mix_kernel, tiles_per_seq=tps),
        grid=(t // tm,),
        in_specs=[pl.BlockSpec((tm, POOL_W), row),
                  pl.BlockSpec((tm, POOL_W), lambda i: (jnp.maximum(i - 1, 0), 0)),
                  pl.BlockSpec((tm, d), row), pl.BlockSpec((tm, d), row), pl.BlockSpec((tm, d), row),
                  pl.BlockSpec((tm, d), row),
                  pl.BlockSpec((1, N_MOD, d), lambda i: (i // tps, 0, 0)),
                  pl.BlockSpec(w_pool.shape, c3),
                  pl.BlockSpec((1, d), c2),
                  pl.BlockSpec((d, d), c2),
                  pl.BlockSpec((1, d), c2),
                  pl.BlockSpec(w_query.shape, c2),
                  pl.BlockSpec(sub_keys.shape, c3)],
        out_specs=(pl.BlockSpec((tm, d), row), pl.BlockSpec((d, tm), lambda i: (0, i)),
                   pl.BlockSpec((n_hp, N_KEYS, tm), lambda i: (0, 0, i))),
        out_shape=(jax.ShapeDtypeStruct((t, d), F32), jax.ShapeDtypeStruct((d, t), BF16),
                   jax.ShapeDtypeStruct((n_hp, N_KEYS, t), F32)),
        scratch_shapes=[pltpu.VMEM((tm, d), BF16)],
        compiler_params=_params(("parallel",)),
        name="mix",
    )(u, u, ga, gb, yb, x2d, mod3, w_pool, pool_scale, w_out, norm2_g, w_query, sub_keys)


RANK_CODE_BASE = -(2.0 ** 127)


def _top16_rows(s):
    cur = s
    rows = []
    for r in range(PEER_TOPK):
        m = jnp.max(cur, axis=0, keepdims=True)
        rows.append(m)
        cur = jnp.where(cur == m, RANK_CODE_BASE * (1.0 + r / 64.0), cur)
    bits = lax.bitcast_convert_type(cur, jnp.int32)
    coded = ((bits >> 17) & 63).astype(F32)
    rank = jnp.where(cur <= RANK_CODE_BASE, coded, float(PEER_TOPK))
    return rows, rank


def _stack_rows(rows, tm):
    n = len(rows)
    rid = lax.broadcasted_iota(jnp.int32, (n, tm), 0)
    arr = jnp.zeros((n, tm), F32)
    for r, v in enumerate(rows):
        arr = jnp.where(rid == r, v, arr)
    return arr


def _candidates(rows1, arr1_hi, rows2, arr2, combine):
    pieces = [combine(rows1[0], arr2)]
    for a in range(1, 8):
        pieces.append(combine(rows1[a], arr2[0:8]))
    pieces.append(combine(arr1_hi, rows2[0]))
    return jnp.concatenate(pieces, axis=0)


def _dup_bf16(x):
    bits = lax.bitcast_convert_type(x.astype(BF16).astype(F32), jnp.uint32)
    return bits | (bits >> 16)


def _row_tile_bf16(ref, h, row, tm):
    words = jnp.broadcast_to(ref[h, pl.ds(row, 1), :], (8, tm))
    return pltpu.bitcast(words, BF16)


def _peer_stats(sc_ref, e1_ref, lim_ref, e2_ref, r2_ref, h):
    tm = sc_ref.shape[2]
    s1 = sc_ref[2 * h]
    s2 = sc_ref[2 * h + 1]
    t1, rank1 = _top16_rows(s1)
    t2, rank2 = _top16_rows(s2)
    t1_hi = _stack_rows(t1[8:], tm)
    t2_arr = _stack_rows(t2, tm)
    cand = _candidates(t1, t1_hi, t2, t2_arr, lambda a, b: a + b)
    n_c = cand.shape[0]
    cid = lax.broadcasted_iota(jnp.int32, (n_c, tm), 0)
    cur = cand
    z = jnp.zeros((1, tm), F32)
    m0 = None
    for r in range(PEER_TOPK):
        m = jnp.max(cur, axis=0, keepdims=True)
        if r == 0:
            m0 = m
        first = jnp.min(jnp.where(cur == m, cid, n_c), axis=0, keepdims=True)
        cur = jnp.where(cid == first, -jnp.inf, cur)
        z = z + jnp.exp(m - m0)
    sel = jnp.where(cur == -jnp.inf, 1.0, 0.0)
    width = [jnp.sum(sel[0:16], axis=0, keepdims=True)]
    for a in range(1, 8):
        width.append(jnp.sum(sel[8 + 8 * a:16 + 8 * a], axis=0, keepdims=True))
    for a in range(8, PEER_TOPK):
        width.append(sel[64 + a:65 + a])
    lim = jnp.zeros_like(s1)
    for a in range(PEER_TOPK):
        lim = jnp.where(rank1 == float(a), width[a], lim)
    inv_z = 1.0 / z
    e1_ref[h] = _dup_bf16(jnp.exp(s1 - t1[0]) * inv_z)
    lim_ref[h] = _dup_bf16(lim)
    e2_ref[h] = jnp.exp(s2 - t2[0]).astype(BF16)
    r2_ref[h] = rank2.astype(BF16)


def _peer_kernel(sc_ref, h2_ref, u_ref, vt_ref, x1_ref, mod_ref, fg_ref, o_ref,
                 e1_ref, lim_ref, e2_ref, r2_ref, acc_ref, a0_ref, a1_ref, y_ref, *, n_blk):
    s = pl.program_id(1)
    n_heads = e1_ref.shape[0]
    eb = u_ref.shape[0]
    rows_per_blk = eb // N_KEYS
    kc = 2 * N_KEYS

    def first_matmul(aw_ref):
        aw_ref[...] = _dot(u_ref[...], h2_ref[...])

    def gate_and_second_matmul(ar_ref):
        tm = ar_ref.shape[1]
        for ii in range(rows_per_blk):
            i_key = (s - 1) * rows_per_blk + ii
            sub = 16
            g = jnp.zeros((N_KEYS // sub, sub, tm), BF16)
            for h in range(n_heads):
                p = _row_tile_bf16(e1_ref, h, i_key, tm)[None] * e2_ref[h].reshape(N_KEYS // sub, sub, tm)
                lim = _row_tile_bf16(lim_ref, h, i_key, tm)[None]
                g = g + jnp.where(r2_ref[h].reshape(N_KEYS // sub, sub, tm) < lim, p, jnp.zeros_like(p))
            g = g.reshape(N_KEYS, tm)
            a = ar_ref[ii * N_KEYS:(ii + 1) * N_KEYS, :].astype(BF16)
            act = (0.5 * a) * (1.0 + lax.erf(a * (0.5 ** 0.5)))
            y_ref[ii * N_KEYS:(ii + 1) * N_KEYS, :] = act * g
            if (ii + 1) * N_KEYS % kc == 0:
                c0 = (ii + 1) * N_KEYS - kc
                acc_ref[...] += _dot(vt_ref[:, c0:c0 + kc], y_ref[c0:c0 + kc, :])

    @pl.when(s == 0)
    def _():
        def head(h, carry):
            _peer_stats(sc_ref, e1_ref, lim_ref, e2_ref, r2_ref, h)
            return carry
        lax.fori_loop(0, n_heads, head, 0)
        acc_ref[...] = jnp.zeros_like(acc_ref)
        first_matmul(a0_ref)

    @pl.when((s % 2 == 1) & (s < n_blk))
    def _():
        first_matmul(a1_ref)
        gate_and_second_matmul(a0_ref)

    @pl.when((s % 2 == 0) & (s > 0) & (s < n_blk))
    def _():
        first_matmul(a0_ref)
        gate_and_second_matmul(a1_ref)

    @pl.when(s == n_blk)
    def _():
        gate_and_second_matmul(a1_ref if n_blk % 2 == 0 else a0_ref)
        peer = acc_ref[...].T
        x2 = x1_ref[...] + mod_ref[0, 5:6, :] * peer
        ms = jnp.mean(x2 * x2, axis=-1, keepdims=True)
        o_ref[...] = x2 * lax.rsqrt(ms + EPS) * fg_ref[...]


def _peer_call(sc, h2, u_bf, vt_bf, x1, mod3, final_g, seq):
    t, d = x1.shape
    tm, eb = PEER_TM, PEER_EB
    tps = seq // tm
    n_hp = sc.shape[0]
    n_exp = u_bf.shape[0]
    row = lambda i, j: (i, 0)
    n_blk = n_exp // eb
    return pl.pallas_call(
        functools.partial(_peer_kernel, n_blk=n_blk),
        grid=(t // tm, n_blk + 1),
        in_specs=[pl.BlockSpec((n_hp, N_KEYS, tm), lambda i, j: (0, 0, i)),
                  pl.BlockSpec((d, tm), lambda i, j: (0, i)),
                  pl.BlockSpec((eb, d), lambda i, j: (jnp.minimum(j, n_blk - 1), 0)),
                  pl.BlockSpec((d, eb), lambda i, j: (0, jnp.maximum(j - 1, 0))),
                  pl.BlockSpec((tm, d), row),
                  pl.BlockSpec((1, N_MOD, d), lambda i, j: (i // tps, 0, 0)),
                  pl.BlockSpec((1, d), lambda i, j: (0, 0))],
        out_specs=pl.BlockSpec((tm, d), row),
        out_shape=jax.ShapeDtypeStruct((t, d), F32),
        scratch_shapes=[pltpu.VMEM((PEER_HEADS, N_KEYS, tm), jnp.uint32),
                        pltpu.VMEM((PEER_HEADS, N_KEYS, tm), jnp.uint32),
                        pltpu.VMEM((PEER_HEADS, N_KEYS, tm), BF16),
                        pltpu.VMEM((PEER_HEADS, N_KEYS, tm), BF16),
                        pltpu.VMEM((d, tm), F32),
                        pltpu.VMEM((eb, tm), F32),
                        pltpu.VMEM((eb, tm), F32),
                        pltpu.VMEM((eb, tm), BF16)],
        compiler_params=_params(("arbitrary", "arbitrary")),
        name="peer",
    )(sc, h2, u_bf, vt_bf, x1, mod3, final_g)


def kernel(x, c, w_mod, b_mod, norm1_g, w_in, b_f, w_pool, pool_scale, w_out, norm2_g,
           peer_w_query, peer_sub_keys, peer_u, peer_v, final_g):
    bsz, seq, d = x.shape
    depth = w_mod.shape[0]
    t = bsz * seq
    assert d == ATTN_HEADS * HEAD_DIM and seq % INPROJ_TM == 0 and t % PEER_TM == 0
    x2d = x.reshape(t, d)
    out = x2d
    for l in range(depth):
        mod3 = _mod_call(c, w_mod[l], b_mod[l]).reshape(bsz, N_MOD, d)

        w = w_in[l]
        o_q, o_k, o_v = POOL_W, POOL_W + d, POOL_W + 2 * d
        o_f = POOL_W + 3 * d
        o_g = o_f + ATTN_HEADS
        w_tok = jnp.concatenate([w[:, :o_q], w[:, o_k:o_v], w[:, o_g:]], axis=1).astype(BF16)
        w_feat_t = jnp.concatenate([w[:, o_q:o_k], w[:, o_v:o_f]], axis=1).T.astype(BF16)
        w_f = jnp.pad(w[:, o_f:o_g], ((0, 0), (0, LANES - ATTN_HEADS))).astype(BF16)
        b_f_row = jnp.pad(b_f[l], (0, LANES - ATTN_HEADS)).reshape(1, LANES)
        u, ka, ga, gb, qt, vt = _inproj_call(
            x2d, mod3, norm1_g[l].reshape(1, d), w_tok, w_feat_t, w_f, b_f_row,
            _forget_selectors(), seq)

        yb = _attn_call(qt, ka, vt, bsz, seq)

        n_hp = 2 * PEER_HEADS
        x1, h2, sc = _mix_call(
            u, ga, gb, yb, x2d, mod3, w_pool[l].astype(BF16), pool_scale[l].reshape(1, d),
            w_out[l].astype(BF16), norm2_g[l].reshape(1, d), peer_w_query[l].astype(BF16),
            peer_sub_keys[l].reshape(n_hp, N_KEYS, -1).astype(BF16), seq)

        assert l == depth - 1 == 0
        out = _peer_call(sc, h2, peer_u[l].astype(BF16), peer_v[l].T.astype(BF16),
                         x1, mod3, final_g.reshape(1, d), seq)
        x2d = out
    return out.reshape(bsz, seq, d)
```

```python
import functools

import jax
import jax.numpy as jnp
from jax import lax
from jax.experimental import pallas as pl
from jax.experimental.pallas import tpu as pltpu

F32 = jnp.float32
BF16 = jnp.bfloat16

EPS = 1e-6
NEG_INF = -1e30
LOG2_E = 1.4426950408889634

POOL_WINDOWS = (2, 4, 8, 16)
POOL_GROUP_W = 128
POOL_W = len(POOL_WINDOWS) * POOL_GROUP_W
ATTN_HEADS = 16
HEAD_DIM = 64
PEER_HEADS = 8
N_KEYS = 128
PEER_TOPK = 16
N_MOD = 6

LANES = 128
VMEM_LIMIT = 56 * 1024 * 1024

INPROJ_TM = 512
ATTN_TQ = 512
ATTN_TK = 512
ATTN_PAIRS = 2
MIX_TM = 256
PEER_TM = 512
PEER_EB = 2048


def _params(sem, flags=None):
    return pltpu.CompilerParams(dimension_semantics=sem, vmem_limit_bytes=VMEM_LIMIT, flags=flags)


def _split3(a):
    hi = a.astype(BF16)
    r1 = a - hi.astype(F32)
    mid = r1.astype(BF16)
    lo = (r1 - mid.astype(F32)).astype(BF16)
    return hi, mid, lo


def _dot(a, b):
    return jnp.dot(a, b, preferred_element_type=F32)


def _dot_nt(a, b):
    return lax.dot_general(a, b, (((1,), (1,)), ((), ())), preferred_element_type=F32)


def _rms_mod(x, g, shift, scale):
    ms = jnp.mean(x * x, axis=-1, keepdims=True)
    y = x * lax.rsqrt(ms + EPS) * g
    return y * (1.0 + scale) + shift


def _mod_kernel(c_ref, w_ref, b_ref, o_ref):
    c_hi, c_mid, _ = _split3(c_ref[...])
    w_hi, w_mid, _ = _split3(w_ref[...])
    acc = _dot(c_hi, w_hi) + _dot(c_hi, w_mid) + _dot(c_mid, w_hi)
    o_ref[...] = acc + b_ref[...]


def _mod_call(c, w_mod, b_mod):
    bsz, d = c.shape
    n = w_mod.shape[1]
    bn = 1024
    return pl.pallas_call(
        _mod_kernel,
        grid=(n // bn,),
        in_specs=[pl.BlockSpec((bsz, d), lambda j: (0, 0)),
                  pl.BlockSpec((d, bn), lambda j: (0, j)),
                  pl.BlockSpec((1, bn), lambda j: (0, j))],
        out_specs=pl.BlockSpec((bsz, bn), lambda j: (0, j)),
        out_shape=jax.ShapeDtypeStruct((bsz, n), F32),
        compiler_params=_params(("parallel",)),
        name="mod",
    )(c, w_mod, b_mod.reshape(1, n))


def _inproj_kernel(x_ref, mod_ref, g_ref, w_ref, wt_ref, wf_ref, bf_ref, sel_ref,
                   u_ref, ka_ref, ga_ref, gb_ref, qt_ref, vt_ref,
                   carry_ref, *, tiles_per_seq, d_model):
    i = pl.program_id(0)
    tm = x_ref.shape[0]
    h = _rms_mod(x_ref[...], g_ref[...], mod_ref[0, 0:1, :], mod_ref[0, 1:2, :])
    hb = h.astype(BF16)

    cw = 512
    n_c = d_model // cw

    def proj(c0):
        return _dot(hb, w_ref[:, c0:c0 + cw])

    u_ref[...] = proj(0).astype(BF16)
    base = POOL_W
    pairs_per_chunk = cw // LANES
    for c in range(n_c):
        kc = proj(base + c * cw).astype(BF16)
        for pp in range(pairs_per_chunk):
            p = c * pairs_per_chunk + pp
            ka_ref[:, p * 2 * LANES:p * 2 * LANES + LANES] = kc[:, pp * LANES:(pp + 1) * LANES]
    base += d_model
    for c in range(n_c):
        ga_ref[:, c * cw:(c + 1) * cw] = jax.nn.sigmoid(proj(base + c * cw)).astype(BF16)
    base += d_model
    for c in range(n_c):
        gb_ref[:, c * cw:(c + 1) * cw] = jax.nn.sigmoid(proj(base + c * cw)).astype(BF16)

    scale = HEAD_DIM ** -0.5 * LOG2_E
    for c in range(n_c):
        qt_ref[c * cw:(c + 1) * cw, :] = (_dot_nt(wt_ref[c * cw:(c + 1) * cw, :], hb) * scale).astype(BF16)
    for c in range(n_c):
        r0 = d_model + c * cw
        vt_ref[c * cw:(c + 1) * cw, :] = _dot_nt(wt_ref[r0:r0 + cw, :], hb).astype(BF16)

    zf = _dot(hb, wf_ref[...]) + bf_ref[...]
    logf = jnp.minimum(zf, 0.0) - jnp.log1p(jnp.exp(-jnp.abs(zf)))
    row = lax.broadcasted_iota(jnp.int32, (tm, tm), 0)
    col = lax.broadcasted_iota(jnp.int32, (tm, tm), 1)
    tril = jnp.where(row >= col, 1.0, 0.0).astype(BF16)
    hi, mid, lo = _split3(logf)
    cs = _dot(tril, hi) + _dot(tril, mid) + _dot(tril, lo)

    @pl.when(i % tiles_per_seq == 0)
    def _():
        carry_ref[...] = jnp.zeros_like(carry_ref)

    f_cum = cs + carry_ref[0:1, :]
    carry_ref[...] = jnp.broadcast_to(f_cum[tm - 1:tm, :], carry_ref.shape)
    n_hi, n_mid, n_lo = _split3(f_cum * -LOG2_E)
    faug = _dot(n_hi, sel_ref[0]) + _dot(n_mid, sel_ref[1]) + _dot(n_lo, sel_ref[2])
    for p in range(d_model // LANES):
        ka_ref[:, p * 2 * LANES + LANES:(p + 1) * 2 * LANES] = faug[:, p * LANES:(p + 1) * LANES].astype(BF16)


def _inproj_call(x2d, mod3, norm_g, w_tok, w_feat_t, w_f, b_f, sel, seq):
    t, d = x2d.shape
    tm = INPROJ_TM
    tps = seq // tm
    row = lambda i: (i, 0)
    colb = lambda i: (0, i)
    c2 = lambda i: (0, 0)
    out_shape = (
        jax.ShapeDtypeStruct((t, POOL_W), BF16),
        jax.ShapeDtypeStruct((t, 2 * d), BF16),
        jax.ShapeDtypeStruct((t, d), BF16), jax.ShapeDtypeStruct((t, d), BF16),
        jax.ShapeDtypeStruct((d, t), BF16), jax.ShapeDtypeStruct((d, t), BF16),
    )
    out_specs = (
        pl.BlockSpec((tm, POOL_W), row),
        pl.BlockSpec((tm, 2 * d), row),
        pl.BlockSpec((tm, d), row), pl.BlockSpec((tm, d), row),
        pl.BlockSpec((d, tm), colb), pl.BlockSpec((d, tm), colb),
    )
    return pl.pallas_call(
        functools.partial(_inproj_kernel, tiles_per_seq=tps, d_model=d),
        grid=(t // tm,),
        in_specs=[pl.BlockSpec((tm, d), row),
                  pl.BlockSpec((1, N_MOD, d), lambda i: (i // tps, 0, 0)),
                  pl.BlockSpec((1, d), c2),
                  pl.BlockSpec(w_tok.shape, c2),
                  pl.BlockSpec(w_feat_t.shape, c2),
                  pl.BlockSpec(w_f.shape, c2),
                  pl.BlockSpec(b_f.shape, c2),
                  pl.BlockSpec(sel.shape, lambda i: (0, 0, 0))],
        out_specs=out_specs,
        out_shape=out_shape,
        scratch_shapes=[pltpu.VMEM((8, LANES), F32)],
        compiler_params=_params(("arbitrary",)),
        name="inproj",
    )(x2d, mod3, norm_g, w_tok, w_feat_t, w_f, b_f, sel)


def _forget_selectors():
    h = jnp.arange(LANES)[:, None]
    c = jnp.arange(ATTN_HEADS // 2 * LANES)[None, :]
    sels = []
    for r in range(3):
        hit = (h < ATTN_HEADS) & (c == (h // 2) * LANES + 3 * (h % 2) + r)
        sels.append(jnp.where(hit, 1.0, 0.0))
    return jnp.stack(sels).astype(BF16)


SUM_ROWS = 16


def _attn_kernel(qt_ref, ka_ref, vt_ref, o_ref, *, tq, tk):
    seq = ka_ref.shape[0]
    rowid = lax.broadcasted_iota(jnp.int32, (LANES, tq), 0)
    krow = lax.broadcasted_iota(jnp.int32, (tk, tq), 0)
    qcol = lax.broadcasted_iota(jnp.int32, (tk, tq), 1)
    aug = [jnp.where((rowid >= 3 * h) & (rowid < 3 * h + 3), 1.0, 0.0).astype(BF16) for h in range(2)]
    head = [jnp.where(rowid // HEAD_DIM == h, 1.0, 0.0).astype(BF16) for h in range(2)]
    ones_rows = jnp.ones((SUM_ROWS, tk), BF16)

    n_heads = qt_ref.shape[0] // HEAD_DIM

    def q_tile(qi, carry):
        q0 = pl.multiple_of(qi * tq, tq)
        ws = []
        for hh in range(n_heads):
            pair, h = hh // 2, hh % 2
            qt = qt_ref[pair * LANES:(pair + 1) * LANES, pl.ds(q0, tq)]
            ws.append(jnp.concatenate([qt * head[h], aug[h]], axis=0))

        def qk(j):
            k0 = pl.multiple_of(j * tk, tk)
            out = []
            for hh in range(n_heads):
                pair = hh // 2
                ka = ka_ref[pl.ds(k0, tk), pair * 2 * LANES:(pair + 1) * 2 * LANES]
                out.append(_dot(ka, ws[hh]))
            return tuple(out)

        def kv_step(j, state, scores, diag):
            k0 = pl.multiple_of(j * tk, tk)
            new = []
            for h in range(n_heads):
                m, acc = state[h]
                s = scores[h]
                if diag is not None:
                    s = jnp.where(krow + diag * tk <= qcol, s, NEG_INF)
                m_new = jnp.maximum(m, jnp.max(s, axis=0, keepdims=True))
                alpha = jnp.exp2(m - m_new)
                p = jnp.exp2(s - m_new)
                vt = vt_ref[h * HEAD_DIM:(h + 1) * HEAD_DIM, pl.ds(k0, tk)]
                acc = alpha * acc + _dot(jnp.concatenate([vt, ones_rows], axis=0), p.astype(BF16))
                new.append((m_new, acc))
            return tuple(new)

        init = tuple((jnp.full((1, tq), NEG_INF, F32), jnp.zeros((HEAD_DIM + SUM_ROWS, tq), F32))
                     for _ in range(n_heads))
        state = lax.fori_loop(0, qi, lambda j, st: kv_step(j, st, qk(j), None), init)
        state = kv_step(qi, state, qk(qi), 0)
        out_t = jnp.concatenate([st[1][:HEAD_DIM] / st[1][HEAD_DIM:HEAD_DIM + 1] for st in state], axis=0)
        o_ref[pl.ds(q0, tq), :] = out_t.T.astype(o_ref.dtype)
        return carry

    lax.fori_loop(0, seq // tq, q_tile, 0)


def _attn_call(qt, ka, vt, bsz, seq):
    d, t = qt.shape
    blk = ATTN_PAIRS * LANES
    return pl.pallas_call(
        functools.partial(_attn_kernel, tq=ATTN_TQ, tk=ATTN_TK),
        grid=(bsz, d // blk),
        in_specs=[pl.BlockSpec((blk, seq), lambda b, p: (p, b)),
                  pl.BlockSpec((seq, 2 * blk), lambda b, p: (b, p)),
                  pl.BlockSpec((blk, seq), lambda b, p: (p, b))],
        out_specs=pl.BlockSpec((seq, blk), lambda b, p: (b, p)),
        out_shape=jax.ShapeDtypeStruct((t, d), BF16),
        compiler_params=_params(("parallel", "parallel")),
        name="attn",
    )(qt, ka, vt)


def _mix_kernel(uc_ref, up_ref, ga_ref, gb_ref, yb_ref, x_ref, mod_ref, wp_ref, ps_ref,
                wo_ref, g2_ref, wq_ref, sk_ref,
                x1_ref, h2_ref, sc_ref, y_ref, *, tiles_per_seq):
    i = pl.program_id(0)
    tm = x_ref.shape[0]
    halo = LANES
    first = (i % tiles_per_seq) == 0
    pos0 = (i % tiles_per_seq) * tm
    r_d = lax.broadcasted_iota(jnp.int32, (tm, tm), 0)
    c_d = lax.broadcasted_iota(jnp.int32, (tm, tm), 1)
    r_o = lax.broadcasted_iota(jnp.int32, (tm, halo), 0)
    c_o = lax.broadcasted_iota(jnp.int32, (tm, halo), 1)
    pos = pos0 + lax.broadcasted_iota(jnp.int32, (tm, 1), 0)
    gw = wp_ref.shape[2]
    for g, w in enumerate(POOL_WINDOWS):
        lo, hi = g * POOL_GROUP_W, (g + 1) * POOL_GROUP_W
        u_cur = uc_ref[:, lo:hi]
        u_prev = up_ref[tm - halo:tm, lo:hi]
        lag = r_d - c_d
        band_d = jnp.where((lag >= 0) & (lag < w), 1.0, 0.0).astype(BF16)
        band_o = jnp.where((r_o + halo - c_o < w) & jnp.logical_not(first), 1.0, 0.0).astype(BF16)
        wsum = _dot(band_d, u_cur) + _dot(band_o, u_prev)
        cnt = jnp.minimum(pos + 1, w).astype(F32)
        pooled = wsum / cnt - u_cur.astype(F32)
        ya = _dot(pooled.astype(BF16), wp_ref[g]) * ps_ref[:, g * gw:(g + 1) * gw]
        sl = slice(g * gw, (g + 1) * gw)
        y = ga_ref[:, sl].astype(F32) * ya + gb_ref[:, sl].astype(F32) * yb_ref[:, sl].astype(F32)
        y_ref[:, sl] = y.astype(BF16)

    o = _dot(y_ref[...], wo_ref[...])
    x1 = x_ref[...] + mod_ref[0, 2:3, :] * o
    x1_ref[...] = x1
    h2f = _rms_mod(x1, g2_ref[...], mod_ref[0, 3:4, :], mod_ref[0, 4:5, :])
    h2 = h2f.astype(BF16)
    h2_ref[...] = h2f.T.astype(BF16)
    n_hp = sk_ref.shape[0]
    half = sk_ref.shape[2]
    qw = 512
    per = qw // half
    for c in range(n_hp // per):
        qp = _dot(h2, wq_ref[:, c * qw:(c + 1) * qw]).astype(BF16)
        for k in range(per):
            hp = c * per + k
            sc_ref[hp] = _dot_nt(sk_ref[hp], qp[:, k * half:(k + 1) * half])


def _mix_call(u, ga, gb, yb, x2d, mod3, w_pool, pool_scale, w_out, norm2_g, w_query, sub_keys, seq):
    t, d = x2d.shape
    tm = MIX_TM
    tps = seq // tm
    row = lambda i: (i, 0)
    c2 = lambda i: (0, 0)
    c3 = lambda i: (0, 0, 0)
    n_hp = sub_keys.shape[0]
    return pl.pallas_call(
        functools.partial(_mix_kernel, tiles_per_seq=tps),
        grid=(t // tm,),
        in_specs=[pl.BlockSpec((tm, POOL_W), row),
                  pl.BlockSpec((tm, POOL_W), lambda i: (jnp.maximum(i - 1, 0), 0)),
                  pl.BlockSpec((tm, d), row), pl.BlockSpec((tm, d), row), pl.BlockSpec((tm, d), row),
                  pl.BlockSpec((tm, d), row),
                  pl.BlockSpec((1, N_MOD, d), lambda i: (i // tps, 0, 0)),
                  pl.BlockSpec(w_pool.shape, c3),
                  pl.BlockSpec((1, d), c2),
                  pl.BlockSpec((d, d), c2),
                  pl.BlockSpec((1, d), c2),
                  pl.BlockSpec(w_query.shape, c2),
                  pl.BlockSpec(sub_keys.shape, c3)],
        out_specs=(pl.BlockSpec((tm, d), row), pl.BlockSpec((d, tm), lambda i: (0, i)),
                   pl.BlockSpec((n_hp, N_KEYS, tm), lambda i: (0, 0, i))),
        out_shape=(jax.ShapeDtypeStruct((t, d), F32), jax.ShapeDtypeStruct((d, t), BF16),
                   jax.ShapeDtypeStruct((n_hp, N_KEYS, t), F32)),
        scratch_shapes=[pltpu.VMEM((tm, d), BF16)],
        compiler_params=_params(("parallel",)),
        name="mix",
    )(u, u, ga, gb, yb, x2d, mod3, w_pool, pool_scale, w_out, norm2_g, w_query, sub_keys)


RANK_CODE_BASE = -(2.0 ** 127)


def _top16_rows(s):
    cur = s
    rows = []
    for r in range(PEER_TOPK):
        m = jnp.max(cur, axis=0, keepdims=True)
        rows.append(m)
        cur = jnp.where(cur == m, RANK_CODE_BASE * (1.0 + r / 64.0), cur)
    bits = lax.bitcast_convert_type(cur, jnp.int32)
    coded = ((bits >> 17) & 63).astype(F32)
    rank = jnp.where(cur <= RANK_CODE_BASE, coded, float(PEER_TOPK))
    return rows, rank


def _stack_rows(rows, tm):
    n = len(rows)
    rid = lax.broadcasted_iota(jnp.int32, (n, tm), 0)
    arr = jnp.zeros((n, tm), F32)
    for r, v in enumerate(rows):
        arr = jnp.where(rid == r, v, arr)
    return arr


def _candidates(rows1, arr1_hi, rows2, arr2, combine):
    pieces = [combine(rows1[0], arr2)]
    for a in range(1, 8):
        pieces.append(combine(rows1[a], arr2[0:8]))
    pieces.append(combine(arr1_hi, rows2[0]))
    return jnp.concatenate(pieces, axis=0)


def _dup_bf16(x):
    bits = lax.bitcast_convert_type(x.astype(BF16).astype(F32), jnp.uint32)
    return bits | (bits >> 16)


def _row_tile_bf16(ref, h, row, tm):
    words = jnp.broadcast_to(ref[h, pl.ds(row, 1), :], (8, tm))
    return pltpu.bitcast(words, BF16)


def _peer_stats(sc_ref, e1_ref, lim_ref, e2_ref, r2_ref, h):
    tm = sc_ref.shape[2]
    s1 = sc_ref[2 * h]
    s2 = sc_ref[2 * h + 1]
    t1, rank1 = _top16_rows(s1)
    t2, rank2 = _top16_rows(s2)
    t1_hi = _stack_rows(t1[8:], tm)
    t2_arr = _stack_rows(t2, tm)
    cand = _candidates(t1, t1_hi, t2, t2_arr, lambda a, b: a + b)
    n_c = cand.shape[0]
    cid = lax.broadcasted_iota(jnp.int32, (n_c, tm), 0)
    cur = cand
    z = jnp.zeros((1, tm), F32)
    m0 = None
    for r in range(PEER_TOPK):
        m = jnp.max(cur, axis=0, keepdims=True)
        if r == 0:
            m0 = m
        first = jnp.min(jnp.where(cur == m, cid, n_c), axis=0, keepdims=True)
        cur = jnp.where(cid == first, -jnp.inf, cur)
        z = z + jnp.exp(m - m0)
    sel = jnp.where(cur == -jnp.inf, 1.0, 0.0)
    width = [jnp.sum(sel[0:16], axis=0, keepdims=True)]
    for a in range(1, 8):
        width.append(jnp.sum(sel[8 + 8 * a:16 + 8 * a], axis=0, keepdims=True))
    for a in range(8, PEER_TOPK):
        width.append(sel[64 + a:65 + a])
    lim = jnp.zeros_like(s1)
    for a in range(PEER_TOPK):
        lim = jnp.where(rank1 == float(a), width[a], lim)
    inv_z = 1.0 / z
    e1_ref[h] = _dup_bf16(jnp.exp(s1 - t1[0]) * inv_z)
    lim_ref[h] = _dup_bf16(lim)
    e2_ref[h] = jnp.exp(s2 - t2[0]).astype(BF16)
    r2_ref[h] = rank2.astype(BF16)


def _peer_kernel(sc_ref, h2_ref, u_ref, vt_ref, x1_ref, mod_ref, fg_ref, o_ref,
                 e1_ref, lim_ref, e2_ref, r2_ref, acc_ref, a0_ref, a1_ref, y_ref, *, n_blk):
    s = pl.program_id(1)
    n_heads = e1_ref.shape[0]
    eb = u_ref.shape[0]
    rows_per_blk = eb // N_KEYS
    kc = 2 * N_KEYS

    def first_matmul(aw_ref):
        aw_ref[...] = _dot(u_ref[...], h2_ref[...])

    def gate_and_second_matmul(ar_ref):
        tm = ar_ref.shape[1]
        for ii in range(rows_per_blk):
            i_key = (s - 1) * rows_per_blk + ii
            sub = 16
            g = None
            for h in range(n_heads):
                p = _row_tile_bf16(e1_ref, h, i_key, tm)[None] * e2_ref[h].reshape(N_KEYS // sub, sub, tm)
                lim = _row_tile_bf16(lim_ref, h, i_key, tm)[None]
                term = jnp.where(r2_ref[h].reshape(N_KEYS // sub, sub, tm) < lim, p, jnp.zeros_like(p))
                g = term if g is None else g + term
            g = g.reshape(N_KEYS, tm)
            a = ar_ref[ii * N_KEYS:(ii + 1) * N_KEYS, :].astype(BF16)
            act = (0.5 * a) * (1.0 + lax.erf(a * (0.5 ** 0.5)))
            y_ref[ii * N_KEYS:(ii + 1) * N_KEYS, :] = act * g
            if (ii + 1) * N_KEYS % kc == 0:
                c0 = (ii + 1) * N_KEYS - kc
                acc_ref[...] += _dot(vt_ref[:, c0:c0 + kc], y_ref[c0:c0 + kc, :])

    @pl.when(s == 0)
    def _():
        def head(h, carry):
            _peer_stats(sc_ref, e1_ref, lim_ref, e2_ref, r2_ref, h)
            return carry
        lax.fori_loop(0, n_heads, head, 0)
        acc_ref[...] = jnp.zeros_like(acc_ref)
        first_matmul(a0_ref)

    @pl.when((s % 2 == 1) & (s < n_blk))
    def _():
        first_matmul(a1_ref)
        gate_and_second_matmul(a0_ref)

    @pl.when((s % 2 == 0) & (s > 0) & (s < n_blk))
    def _():
        first_matmul(a0_ref)
        gate_and_second_matmul(a1_ref)

    @pl.when(s == n_blk)
    def _():
        gate_and_second_matmul(a1_ref if n_blk % 2 == 0 else a0_ref)
        peer = acc_ref[...].T
        x2 = x1_ref[...] + mod_ref[0, 5:6, :] * peer
        ms = jnp.mean(x2 * x2, axis=-1, keepdims=True)
        o_ref[...] = x2 * lax.rsqrt(ms + EPS) * fg_ref[...]


def _peer_call(sc, h2, u_bf, vt_bf, x1, mod3, final_g, seq):
    t, d = x1.shape
    tm, eb = PEER_TM, PEER_EB
    tps = seq // tm
    n_hp = sc.shape[0]
    n_exp = u_bf.shape[0]
    row = lambda i, j: (i, 0)
    n_blk = n_exp // eb
    return pl.pallas_call(
        functools.partial(_peer_kernel, n_blk=n_blk),
        grid=(t // tm, n_blk + 1),
        in_specs=[pl.BlockSpec((n_hp, N_KEYS, tm), lambda i, j: (0, 0, i)),
                  pl.BlockSpec((d, tm), lambda i, j: (0, i)),
                  pl.BlockSpec((eb, d), lambda i, j: (jnp.minimum(j, n_blk - 1), 0)),
                  pl.BlockSpec((d, eb), lambda i, j: (0, jnp.maximum(j - 1, 0))),
                  pl.BlockSpec((tm, d), row),
                  pl.BlockSpec((1, N_MOD, d), lambda i, j: (i // tps, 0, 0)),
                  pl.BlockSpec((1, d), lambda i, j: (0, 0))],
        out_specs=pl.BlockSpec((tm, d), row),
        out_shape=jax.ShapeDtypeStruct((t, d), F32),
        scratch_shapes=[pltpu.VMEM((PEER_HEADS, N_KEYS, tm), jnp.uint32),
                        pltpu.VMEM((PEER_HEADS, N_KEYS, tm), jnp.uint32),
                        pltpu.VMEM((PEER_HEADS, N_KEYS, tm), BF16),
                        pltpu.VMEM((PEER_HEADS, N_KEYS, tm), BF16),
                        pltpu.VMEM((d, tm), F32),
                        pltpu.VMEM((eb, tm), F32),
                        pltpu.VMEM((eb, tm), F32),
                        pltpu.VMEM((eb, tm), BF16)],
        compiler_params=_params(("arbitrary", "arbitrary")),
        name="peer",
    )(sc, h2, u_bf, vt_bf, x1, mod3, final_g)


def kernel(x, c, w_mod, b_mod, norm1_g, w_in, b_f, w_pool, pool_scale, w_out, norm2_g,
           peer_w_query, peer_sub_keys, peer_u, peer_v, final_g):
    bsz, seq, d = x.shape
    depth = w_mod.shape[0]
    t = bsz * seq
    assert d == ATTN_HEADS * HEAD_DIM and seq % INPROJ_TM == 0 and t % PEER_TM == 0
    x2d = x.reshape(t, d)
    out = x2d
    for l in range(depth):
        mod3 = _mod_call(c, w_mod[l], b_mod[l]).reshape(bsz, N_MOD, d)

        w = w_in[l]
        o_q, o_k, o_v = POOL_W, POOL_W + d, POOL_W + 2 * d
        o_f = POOL_W + 3 * d
        o_g = o_f + ATTN_HEADS
        w_tok = jnp.concatenate([w[:, :o_q], w[:, o_k:o_v], w[:, o_g:]], axis=1).astype(BF16)
        w_feat_t = jnp.concatenate([w[:, o_q:o_k], w[:, o_v:o_f]], axis=1).T.astype(BF16)
        w_f = jnp.pad(w[:, o_f:o_g], ((0, 0), (0, LANES - ATTN_HEADS))).astype(BF16)
        b_f_row = jnp.pad(b_f[l], (0, LANES - ATTN_HEADS)).reshape(1, LANES)
        u, ka, ga, gb, qt, vt = _inproj_call(
            x2d, mod3, norm1_g[l].reshape(1, d), w_tok, w_feat_t, w_f, b_f_row,
            _forget_selectors(), seq)

        yb = _attn_call(qt, ka, vt, bsz, seq)

        n_hp = 2 * PEER_HEADS
        x1, h2, sc = _mix_call(
            u, ga, gb, yb, x2d, mod3, w_pool[l].astype(BF16), pool_scale[l].reshape(1, d),
            w_out[l].astype(BF16), norm2_g[l].reshape(1, d), peer_w_query[l].astype(BF16),
            peer_sub_keys[l].reshape(n_hp, N_KEYS, -1).astype(BF16), seq)

        assert l == depth - 1 == 0
        out = _peer_call(sc, h2, peer_u[l].astype(BF16), peer_v[l].T.astype(BF16),
                         x1, mod3, final_g.reshape(1, d), seq)
        x2d = out
    return out.reshape(bsz, seq, d)
```

```python
import functools

import jax
import jax.numpy as jnp
from jax import lax
from jax.experimental import pallas as pl
from jax.experimental.pallas import tpu as pltpu

F32 = jnp.float32
BF16 = jnp.bfloat16

EPS = 1e-6
NEG_INF = -1e30
LOG2_E = 1.4426950408889634

POOL_WINDOWS = (2, 4, 8, 16)
POOL_GROUP_W = 128
POOL_W = len(POOL_WINDOWS) * POOL_GROUP_W
ATTN_HEADS = 16
HEAD_DIM = 64
PEER_HEADS = 8
N_KEYS = 128
PEER_TOPK = 16
N_MOD = 6

LANES = 128
VMEM_LIMIT = 56 * 1024 * 1024

INPROJ_TM = 512
ATTN_TQ = 512
ATTN_TK = 512
ATTN_PAIRS = 2
MIX_TM = 256
PEER_TM = 512
PEER_EB = 2048


def _params(sem, flags=None):
    return pltpu.CompilerParams(dimension_semantics=sem, vmem_limit_bytes=VMEM_LIMIT, flags=flags)


def _split3(a):
    hi = a.astype(BF16)
    r1 = a - hi.astype(F32)
    mid = r1.astype(BF16)
    lo = (r1 - mid.astype(F32)).astype(BF16)
    return hi, mid, lo


def _dot(a, b):
    return jnp.dot(a, b, preferred_element_type=F32)


def _dot_nt(a, b):
    return lax.dot_general(a, b, (((1,), (1,)), ((), ())), preferred_element_type=F32)


def _rms_mod(x, g, shift, scale):
    ms = jnp.mean(x * x, axis=-1, keepdims=True)
    y = x * lax.rsqrt(ms + EPS) * g
    return y * (1.0 + scale) + shift


def _mod_kernel(c_ref, w_ref, b_ref, o_ref):
    c_hi, c_mid, _ = _split3(c_ref[...])
    w_hi, w_mid, _ = _split3(w_ref[...])
    acc = _dot(c_hi, w_hi) + _dot(c_hi, w_mid) + _dot(c_mid, w_hi)
    o_ref[...] = acc + b_ref[...]


def _mod_call(c, w_mod, b_mod):
    bsz, d = c.shape
    n = w_mod.shape[1]
    bn = 1024
    return pl.pallas_call(
        _mod_kernel,
        grid=(n // bn,),
        in_specs=[pl.BlockSpec((bsz, d), lambda j: (0, 0)),
                  pl.BlockSpec((d, bn), lambda j: (0, j)),
                  pl.BlockSpec((1, bn), lambda j: (0, j))],
        out_specs=pl.BlockSpec((bsz, bn), lambda j: (0, j)),
        out_shape=jax.ShapeDtypeStruct((bsz, n), F32),
        compiler_params=_params(("parallel",)),
        name="mod",
    )(c, w_mod, b_mod.reshape(1, n))


def _inproj_kernel(x_ref, mod_ref, g_ref, w_ref, wt_ref, wf_ref, bf_ref, sel_ref,
                   u_ref, ka_ref, ga_ref, gb_ref, qt_ref, vt_ref,
                   carry_ref, *, tiles_per_seq, d_model):
    i = pl.program_id(0)
    tm = x_ref.shape[0]
    h = _rms_mod(x_ref[...], g_ref[...], mod_ref[0, 0:1, :], mod_ref[0, 1:2, :])
    hb = h.astype(BF16)

    cw = 512
    n_c = d_model // cw

    def proj(c0):
        return _dot(hb, w_ref[:, c0:c0 + cw])

    u_ref[...] = proj(0).astype(BF16)
    base = POOL_W
    pairs_per_chunk = cw // LANES
    for c in range(n_c):
        kc = proj(base + c * cw).astype(BF16)
        for pp in range(pairs_per_chunk):
            p = c * pairs_per_chunk + pp
            ka_ref[:, p * 2 * LANES:p * 2 * LANES + LANES] = kc[:, pp * LANES:(pp + 1) * LANES]
    base += d_model
    for c in range(n_c):
        ga_ref[:, c * cw:(c + 1) * cw] = jax.nn.sigmoid(proj(base + c * cw)).astype(BF16)
    base += d_model
    for c in range(n_c):
        gb_ref[:, c * cw:(c + 1) * cw] = jax.nn.sigmoid(proj(base + c * cw)).astype(BF16)

    scale = HEAD_DIM ** -0.5 * LOG2_E
    for c in range(n_c):
        qt_ref[c * cw:(c + 1) * cw, :] = (_dot_nt(wt_ref[c * cw:(c + 1) * cw, :], hb) * scale).astype(BF16)
    for c in range(n_c):
        r0 = d_model + c * cw
        vt_ref[c * cw:(c + 1) * cw, :] = _dot_nt(wt_ref[r0:r0 + cw, :], hb).astype(BF16)

    zf = _dot(hb, wf_ref[...]) + bf_ref[...]
    logf = jnp.minimum(zf, 0.0) - jnp.log1p(jnp.exp(-jnp.abs(zf)))
    row = lax.broadcasted_iota(jnp.int32, (tm, tm), 0)
    col = lax.broadcasted_iota(jnp.int32, (tm, tm), 1)
    tril = jnp.where(row >= col, 1.0, 0.0).astype(BF16)
    hi, mid, lo = _split3(logf)
    cs = _dot(tril, hi) + _dot(tril, mid) + _dot(tril, lo)

    @pl.when(i % tiles_per_seq == 0)
    def _():
        carry_ref[...] = jnp.zeros_like(carry_ref)

    f_cum = cs + carry_ref[0:1, :]
    carry_ref[...] = jnp.broadcast_to(f_cum[tm - 1:tm, :], carry_ref.shape)
    n_hi, n_mid, n_lo = _split3(f_cum * -LOG2_E)
    faug = _dot(n_hi, sel_ref[0]) + _dot(n_mid, sel_ref[1]) + _dot(n_lo, sel_ref[2])
    for p in range(d_model // LANES):
        ka_ref[:, p * 2 * LANES + LANES:(p + 1) * 2 * LANES] = faug[:, p * LANES:(p + 1) * LANES].astype(BF16)


def _inproj_call(x2d, mod3, norm_g, w_tok, w_feat_t, w_f, b_f, sel, seq):
    t, d = x2d.shape
    tm = INPROJ_TM
    tps = seq // tm
    row = lambda i: (i, 0)
    colb = lambda i: (0, i)
    c2 = lambda i: (0, 0)
    out_shape = (
        jax.ShapeDtypeStruct((t, POOL_W), BF16),
        jax.ShapeDtypeStruct((t, 2 * d), BF16),
        jax.ShapeDtypeStruct((t, d), BF16), jax.ShapeDtypeStruct((t, d), BF16),
        jax.ShapeDtypeStruct((d, t), BF16), jax.ShapeDtypeStruct((d, t), BF16),
    )
    out_specs = (
        pl.BlockSpec((tm, POOL_W), row),
        pl.BlockSpec((tm, 2 * d), row),
        pl.BlockSpec((tm, d), row), pl.BlockSpec((tm, d), row),
        pl.BlockSpec((d, tm), colb), pl.BlockSpec((d, tm), colb),
    )
    return pl.pallas_call(
        functools.partial(_inproj_kernel, tiles_per_seq=tps, d_model=d),
        grid=(t // tm,),
        in_specs=[pl.BlockSpec((tm, d), row),
                  pl.BlockSpec((1, N_MOD, d), lambda i: (i // tps, 0, 0)),
                  pl.BlockSpec((1, d), c2),
                  pl.BlockSpec(w_tok.shape, c2),
                  pl.BlockSpec(w_feat_t.shape, c2),
                  pl.BlockSpec(w_f.shape, c2),
                  pl.BlockSpec(b_f.shape, c2),
                  pl.BlockSpec(sel.shape, lambda i: (0, 0, 0))],
        out_specs=out_specs,
        out_shape=out_shape,
        scratch_shapes=[pltpu.VMEM((8, LANES), F32)],
        compiler_params=_params(("arbitrary",)),
        name="inproj",
    )(x2d, mod3, norm_g, w_tok, w_feat_t, w_f, b_f, sel)


def _forget_selectors():
    h = jnp.arange(LANES)[:, None]
    c = jnp.arange(ATTN_HEADS // 2 * LANES)[None, :]
    sels = []
    for r in range(3):
        hit = (h < ATTN_HEADS) & (c == (h // 2) * LANES + 3 * (h % 2) + r)
        sels.append(jnp.where(hit, 1.0, 0.0))
    return jnp.stack(sels).astype(BF16)


SUM_ROWS = 16


def _attn_kernel(qt_ref, ka_ref, vt_ref, o_ref, *, tq, tk):
    seq = ka_ref.shape[0]
    rowid = lax.broadcasted_iota(jnp.int32, (LANES, tq), 0)
    krow = lax.broadcasted_iota(jnp.int32, (tk, tq), 0)
    qcol = lax.broadcasted_iota(jnp.int32, (tk, tq), 1)
    aug = [jnp.where((rowid >= 3 * h) & (rowid < 3 * h + 3), 1.0, 0.0).astype(BF16) for h in range(2)]
    head = [jnp.where(rowid // HEAD_DIM == h, 1.0, 0.0).astype(BF16) for h in range(2)]
    ones_rows = jnp.ones((SUM_ROWS, tk), BF16)

    n_heads = qt_ref.shape[0] // HEAD_DIM

    def q_tile(qi, carry):
        q0 = pl.multiple_of(qi * tq, tq)
        ws = []
        for hh in range(n_heads):
            pair, h = hh // 2, hh % 2
            qt = qt_ref[pair * LANES:(pair + 1) * LANES, pl.ds(q0, tq)]
            ws.append(jnp.concatenate([qt * head[h], aug[h]], axis=0))

        def qk(j):
            k0 = pl.multiple_of(j * tk, tk)
            out = []
            for hh in range(n_heads):
                pair = hh // 2
                ka = ka_ref[pl.ds(k0, tk), pair * 2 * LANES:(pair + 1) * 2 * LANES]
                out.append(_dot(ka, ws[hh]))
            return tuple(out)

        def kv_step(j, state, scores, diag):
            k0 = pl.multiple_of(j * tk, tk)
            new = []
            for h in range(n_heads):
                m, acc = state[h]
                s = scores[h]
                if diag is not None:
                    s = jnp.where(krow + diag * tk <= qcol, s, NEG_INF)
                m_new = jnp.maximum(m, jnp.max(s, axis=0, keepdims=True))
                alpha = jnp.exp2(m - m_new)
                p = jnp.exp2(s - m_new)
                vt = vt_ref[h * HEAD_DIM:(h + 1) * HEAD_DIM, pl.ds(k0, tk)]
                acc = alpha * acc + _dot(jnp.concatenate([vt, ones_rows], axis=0), p.astype(BF16))
                new.append((m_new, acc))
            return tuple(new)

        init = tuple((jnp.full((1, tq), NEG_INF, F32), jnp.zeros((HEAD_DIM + SUM_ROWS, tq), F32))
                     for _ in range(n_heads))
        state = lax.fori_loop(0, qi, lambda j, st: kv_step(j, st, qk(j), None), init)
        state = kv_step(qi, state, qk(qi), 0)
        out_t = jnp.concatenate([st[1][:HEAD_DIM] / st[1][HEAD_DIM:HEAD_DIM + 1] for st in state], axis=0)
        o_ref[pl.ds(q0, tq), :] = out_t.T.astype(o_ref.dtype)
        return carry

    lax.fori_loop(0, seq // tq, q_tile, 0)


def _attn_call(qt, ka, vt, bsz, seq):
    d, t = qt.shape
    blk = ATTN_PAIRS * LANES
    return pl.pallas_call(
        functools.partial(_attn_kernel, tq=ATTN_TQ, tk=ATTN_TK),
        grid=(bsz, d // blk),
        in_specs=[pl.BlockSpec((blk, seq), lambda b, p: (p, b)),
                  pl.BlockSpec((seq, 2 * blk), lambda b, p: (b, p)),
                  pl.BlockSpec((blk, seq), lambda b, p: (p, b))],
        out_specs=pl.BlockSpec((seq, blk), lambda b, p: (b, p)),
        out_shape=jax.ShapeDtypeStruct((t, d), BF16),
        compiler_params=_params(("parallel", "parallel")),
        name="attn",
    )(qt, ka, vt)


def _mix_kernel(uc_ref, up_ref, ga_ref, gb_ref, yb_ref, x_ref, mod_ref, wp_ref, ps_ref,
                wo_ref, g2_ref, wq_ref, sk_ref,
                x1_ref, h2_ref, sc_ref, y_ref, *, tiles_per_seq):
    i = pl.program_id(0)
    tm = x_ref.shape[0]
    halo = LANES
    first = (i % tiles_per_seq) == 0
    pos0 = (i % tiles_per_seq) * tm
    r_d = lax.broadcasted_iota(jnp.int32, (tm, tm), 0)
    c_d = lax.broadcasted_iota(jnp.int32, (tm, tm), 1)
    r_o = lax.broadcasted_iota(jnp.int32, (tm, halo), 0)
    c_o = lax.broadcasted_iota(jnp.int32, (tm, halo), 1)
    pos = pos0 + lax.broadcasted_iota(jnp.int32, (tm, 1), 0)
    gw = wp_ref.shape[2]
    for g, w in enumerate(POOL_WINDOWS):
        lo, hi = g * POOL_GROUP_W, (g + 1) * POOL_GROUP_W
        u_cur = uc_ref[:, lo:hi]
        u_prev = up_ref[tm - halo:tm, lo:hi]
        lag = r_d - c_d
        band_d = jnp.where((lag >= 0) & (lag < w), 1.0, 0.0).astype(BF16)
        band_o = jnp.where((r_o + halo - c_o < w) & jnp.logical_not(first), 1.0, 0.0).astype(BF16)
        wsum = _dot(band_d, u_cur) + _dot(band_o, u_prev)
        cnt = jnp.minimum(pos + 1, w).astype(F32)
        pooled = wsum / cnt - u_cur.astype(F32)
        ya = _dot(pooled.astype(BF16), wp_ref[g]) * ps_ref[:, g * gw:(g + 1) * gw]
        sl = slice(g * gw, (g + 1) * gw)
        y = ga_ref[:, sl].astype(F32) * ya + gb_ref[:, sl].astype(F32) * yb_ref[:, sl].astype(F32)
        y_ref[:, sl] = y.astype(BF16)

    o = _dot(y_ref[...], wo_ref[...])
    x1 = x_ref[...] + mod_ref[0, 2:3, :] * o
    x1_ref[...] = x1
    h2f = _rms_mod(x1, g2_ref[...], mod_ref[0, 3:4, :], mod_ref[0, 4:5, :])
    h2 = h2f.astype(BF16)
    h2_ref[...] = h2f.T.astype(BF16)
    n_hp = sk_ref.shape[0]
    half = sk_ref.shape[2]
    qw = 512
    per = qw // half
    for c in range(n_hp // per):
        qp = _dot(h2, wq_ref[:, c * qw:(c + 1) * qw]).astype(BF16)
        for k in range(per):
            hp = c * per + k
            sc_ref[hp] = _dot_nt(sk_ref[hp], qp[:, k * half:(k + 1) * half])


def _mix_call(u, ga, gb, yb, x2d, mod3, w_pool, pool_scale, w_out, norm2_g, w_query, sub_keys, seq):
    t, d = x2d.shape
    tm = MIX_TM
    tps = seq // tm
    row = lambda i: (i, 0)
    c2 = lambda i: (0, 0)
    c3 = lambda i: (0, 0, 0)
    n_hp = sub_keys.shape[0]
    return pl.pallas_call(
        functools.partial(_mix_kernel, tiles_per_seq=tps),
        grid=(t // tm,),
        in_specs=[pl.BlockSpec((tm, POOL_W), row),
                  pl.BlockSpec((tm, POOL_W), lambda i: (jnp.maximum(i - 1, 0), 0)),
                  pl.BlockSpec((tm, d), row), pl.BlockSpec((tm, d), row), pl.BlockSpec((tm, d), row),
                  pl.BlockSpec((tm, d), row),
                  pl.BlockSpec((1, N_MOD, d), lambda i: (i // tps, 0, 0)),
                  pl.BlockSpec(w_pool.shape, c3),
                  pl.BlockSpec((1, d), c2),
                  pl.BlockSpec((d, d), c2),
                  pl.BlockSpec((1, d), c2),
                  pl.BlockSpec(w_query.shape, c2),
                  pl.BlockSpec(sub_keys.shape, c3)],
        out_specs=(pl.BlockSpec((tm, d), row), pl.BlockSpec((d, tm), lambda i: (0, i)),
                   pl.BlockSpec((n_hp, N_KEYS, tm), lambda i: (0, 0, i))),
        out_shape=(jax.ShapeDtypeStruct((t, d), F32), jax.ShapeDtypeStruct((d, t), BF16),
                   jax.ShapeDtypeStruct((n_hp, N_KEYS, t), F32)),
        scratch_shapes=[pltpu.VMEM((tm, d), BF16)],
        compiler_params=_params(("parallel",)),
        name="mix",
    )(u, u, ga, gb, yb, x2d, mod3, w_pool, pool_scale, w_out, norm2_g, w_query, sub_keys)


RANK_CODE_BASE = -(2.0 ** 127)


def _top16_rows(s):
    cur = s
    rows = []
    for r in range(PEER_TOPK):
        m = jnp.max(cur, axis=0, keepdims=True)
        rows.append(m)
        cur = jnp.where(cur == m, RANK_CODE_BASE * (1.0 + r / 64.0), cur)
    bits = lax.bitcast_convert_type(cur, jnp.int32)
    coded = ((bits >> 17) & 63).astype(F32)
    rank = jnp.where(cur <= RANK_CODE_BASE, coded, float(PEER_TOPK))
    return rows, rank


def _stack_rows(rows, tm):
    n = len(rows)
    rid = lax.broadcasted_iota(jnp.int32, (n, tm), 0)
    arr = jnp.zeros((n, tm), F32)
    for r, v in enumerate(rows):
        arr = jnp.where(rid == r, v, arr)
    return arr


def _candidates(rows1, arr1_hi, rows2, arr2, combine):
    pieces = [combine(rows1[0], arr2)]
    for a in range(1, 8):
        pieces.append(combine(rows1[a], arr2[0:8]))
    pieces.append(combine(arr1_hi, rows2[0]))
    return jnp.concatenate(pieces, axis=0)


def _dup_bf16(x):
    bits = lax.bitcast_convert_type(x.astype(BF16).astype(F32), jnp.uint32)
    return bits | (bits >> 16)


def _row_tile_bf16(ref, h, row, tm):
    words = jnp.broadcast_to(ref[h, pl.ds(row, 1), :], (8, tm))
    return pltpu.bitcast(words, BF16)


def _peer_stats(sc_ref, e1_ref, lim_ref, e2_ref, r2_ref, h):
    tm = sc_ref.shape[2]
    s1 = sc_ref[2 * h]
    s2 = sc_ref[2 * h + 1]
    t1, rank1 = _top16_rows(s1)
    t2, rank2 = _top16_rows(s2)
    t1_hi = _stack_rows(t1[8:], tm)
    t2_arr = _stack_rows(t2, tm)
    cand = _candidates(t1, t1_hi, t2, t2_arr, lambda a, b: a + b)
    n_c = cand.shape[0]
    cid = lax.broadcasted_iota(jnp.int32, (n_c, tm), 0)
    cur = cand
    z = jnp.zeros((1, tm), F32)
    m0 = None
    for r in range(PEER_TOPK):
        m = jnp.max(cur, axis=0, keepdims=True)
        if r == 0:
            m0 = m
        first = jnp.min(jnp.where(cur == m, cid, n_c), axis=0, keepdims=True)
        cur = jnp.where(cid == first, -jnp.inf, cur)
        z = z + jnp.exp(m - m0)
    sel = jnp.where(cur == -jnp.inf, 1.0, 0.0)
    width = [jnp.sum(sel[0:16], axis=0, keepdims=True)]
    for a in range(1, 8):
        width.append(jnp.sum(sel[8 + 8 * a:16 + 8 * a], axis=0, keepdims=True))
    for a in range(8, PEER_TOPK):
        width.append(sel[64 + a:65 + a])
    lim = jnp.zeros_like(s1)
    for a in range(PEER_TOPK):
        lim = jnp.where(rank1 == float(a), width[a], lim)
    inv_z = 1.0 / z
    e1_ref[h] = _dup_bf16(jnp.exp(s1 - t1[0]) * inv_z)
    lim_ref[h] = _dup_bf16(lim)
    e2_ref[h] = jnp.exp(s2 - t2[0]).astype(BF16)
    r2_ref[h] = rank2.astype(BF16)


def _peer_kernel(sc_ref, h2_ref, u_ref, vt_ref, x1_ref, mod_ref, fg_ref, o_ref,
                 e1_ref, lim_ref, e2_ref, r2_ref, acc_ref, y_ref):
    s = pl.program_id(1)
    n_heads = e1_ref.shape[0]
    eb = u_ref.shape[0]
    tm = h2_ref.shape[1]
    rows_per_blk = eb // N_KEYS
    kc = 2 * N_KEYS

    @pl.when(s == 0)
    def _():
        def head(h, carry):
            _peer_stats(sc_ref, e1_ref, lim_ref, e2_ref, r2_ref, h)
            return carry
        lax.fori_loop(0, n_heads, head, 0)
        acc_ref[...] = jnp.zeros_like(acc_ref)

    def expert_block():
        a_chunks = [_dot(u_ref[c * kc:(c + 1) * kc, :], h2_ref[...]) for c in range(eb // kc)]
        for ii in range(rows_per_blk):
            i_key = s * rows_per_blk + ii
            sub = 16
            g = None
            for h in range(n_heads):
                p = _row_tile_bf16(e1_ref, h, i_key, tm)[None] * e2_ref[h].reshape(N_KEYS // sub, sub, tm)
                lim = _row_tile_bf16(lim_ref, h, i_key, tm)[None]
                term = jnp.where(r2_ref[h].reshape(N_KEYS // sub, sub, tm) < lim, p, jnp.zeros_like(p))
                g = term if g is None else g + term
            g = g.reshape(N_KEYS, tm)
            r_in = ii * N_KEYS % kc
            a = a_chunks[ii * N_KEYS // kc][r_in:r_in + N_KEYS].astype(BF16)
            act = (0.5 * a) * (1.0 + lax.erf(a * (0.5 ** 0.5)))
            y_ref[ii * N_KEYS:(ii + 1) * N_KEYS, :] = act * g
            if (ii + 1) * N_KEYS % kc == 0:
                c0 = (ii + 1) * N_KEYS - kc
                acc_ref[...] += _dot(vt_ref[:, c0:c0 + kc], y_ref[c0:c0 + kc, :])

    expert_block()

    @pl.when(s == pl.num_programs(1) - 1)
    def _():
        peer = acc_ref[...].T
        x2 = x1_ref[...] + mod_ref[0, 5:6, :] * peer
        ms = jnp.mean(x2 * x2, axis=-1, keepdims=True)
        o_ref[...] = x2 * lax.rsqrt(ms + EPS) * fg_ref[...]


def _peer_call(sc, h2, u_bf, vt_bf, x1, mod3, final_g, seq):
    t, d = x1.shape
    tm, eb = PEER_TM, PEER_EB
    tps = seq // tm
    n_hp = sc.shape[0]
    n_exp = u_bf.shape[0]
    row = lambda i, j: (i, 0)
    return pl.pallas_call(
        _peer_kernel,
        grid=(t // tm, n_exp // eb),
        in_specs=[pl.BlockSpec((n_hp, N_KEYS, tm), lambda i, j: (0, 0, i)),
                  pl.BlockSpec((d, tm), lambda i, j: (0, i)),
                  pl.BlockSpec((eb, d), lambda i, j: (j, 0)),
                  pl.BlockSpec((d, eb), lambda i, j: (0, j)),
                  pl.BlockSpec((tm, d), row),
                  pl.BlockSpec((1, N_MOD, d), lambda i, j: (i // tps, 0, 0)),
                  pl.BlockSpec((1, d), lambda i, j: (0, 0))],
        out_specs=pl.BlockSpec((tm, d), row),
        out_shape=jax.ShapeDtypeStruct((t, d), F32),
        scratch_shapes=[pltpu.VMEM((PEER_HEADS, N_KEYS, tm), jnp.uint32),
                        pltpu.VMEM((PEER_HEADS, N_KEYS, tm), jnp.uint32),
                        pltpu.VMEM((PEER_HEADS, N_KEYS, tm), BF16),
                        pltpu.VMEM((PEER_HEADS, N_KEYS, tm), BF16),
                        pltpu.VMEM((d, tm), F32),
                        pltpu.VMEM((eb, tm), BF16)],
        compiler_params=_params(("parallel", "arbitrary")),
        name="peer",
    )(sc, h2, u_bf, vt_bf, x1, mod3, final_g)


def kernel(x, c, w_mod, b_mod, norm1_g, w_in, b_f, w_pool, pool_scale, w_out, norm2_g,
           peer_w_query, peer_sub_keys, peer_u, peer_v, final_g):
    bsz, seq, d = x.shape
    depth = w_mod.shape[0]
    t = bsz * seq
    assert d == ATTN_HEADS * HEAD_DIM and seq % INPROJ_TM == 0 and t % PEER_TM == 0
    x2d = x.reshape(t, d)
    out = x2d
    for l in range(depth):
        mod3 = _mod_call(c, w_mod[l], b_mod[l]).reshape(bsz, N_MOD, d)

        w = w_in[l]
        o_q, o_k, o_v = POOL_W, POOL_W + d, POOL_W + 2 * d
        o_f = POOL_W + 3 * d
        o_g = o_f + ATTN_HEADS
        w_tok = jnp.concatenate([w[:, :o_q], w[:, o_k:o_v], w[:, o_g:]], axis=1).astype(BF16)
        w_feat_t = jnp.concatenate([w[:, o_q:o_k], w[:, o_v:o_f]], axis=1).T.astype(BF16)
        w_f = jnp.pad(w[:, o_f:o_g], ((0, 0), (0, LANES - ATTN_HEADS))).astype(BF16)
        b_f_row = jnp.pad(b_f[l], (0, LANES - ATTN_HEADS)).reshape(1, LANES)
        u, ka, ga, gb, qt, vt = _inproj_call(
            x2d, mod3, norm1_g[l].reshape(1, d), w_tok, w_feat_t, w_f, b_f_row,
            _forget_selectors(), seq)

        yb = _attn_call(qt, ka, vt, bsz, seq)

        n_hp = 2 * PEER_HEADS
        x1, h2, sc = _mix_call(
            u, ga, gb, yb, x2d, mod3, w_pool[l].astype(BF16), pool_scale[l].reshape(1, d),
            w_out[l].astype(BF16), norm2_g[l].reshape(1, d), peer_w_query[l].astype(BF16),
            peer_sub_keys[l].reshape(n_hp, N_KEYS, -1).astype(BF16), seq)

        assert l == depth - 1 == 0
        out = _peer_call(sc, h2, peer_u[l].astype(BF16), peer_v[l].T.astype(BF16),
                         x1, mod3, final_g.reshape(1, d), seq)
        x2d = out
    return out.reshape(bsz, seq, d)
```

```python
import functools

import jax
import jax.numpy as jnp
from jax import lax
from jax.experimental import pallas as pl
from jax.experimental.pallas import tpu as pltpu

F32 = jnp.float32
BF16 = jnp.bfloat16

EPS = 1e-6
NEG_INF = -1e30
LOG2_E = 1.4426950408889634

POOL_WINDOWS = (2, 4, 8, 16)
POOL_GROUP_W = 128
POOL_W = len(POOL_WINDOWS) * POOL_GROUP_W
ATTN_HEADS = 16
HEAD_DIM = 64
PEER_HEADS = 8
N_KEYS = 128
PEER_TOPK = 16
N_MOD = 6

LANES = 128
VMEM_LIMIT = 56 * 1024 * 1024

INPROJ_TM = 512
ATTN_TQ = 512
ATTN_TK = 512
ATTN_PAIRS = 4
MIX_TM = 512
PEER_TM = 512
PEER_EB = 2048


def _params(sem, flags=None):
    return pltpu.CompilerParams(dimension_semantics=sem, vmem_limit_bytes=VMEM_LIMIT, flags=flags)


def _split3(a):
    hi = a.astype(BF16)
    r1 = a - hi.astype(F32)
    mid = r1.astype(BF16)
    lo = (r1 - mid.astype(F32)).astype(BF16)
    return hi, mid, lo


def _dot(a, b):
    return jnp.dot(a, b, preferred_element_type=F32)


def _dot_nt(a, b):
    return lax.dot_general(a, b, (((1,), (1,)), ((), ())), preferred_element_type=F32)


def _rms_mod(x, g, shift, scale):
    ms = jnp.mean(x * x, axis=-1, keepdims=True)
    y = x * lax.rsqrt(ms + EPS) * g
    return y * (1.0 + scale) + shift


def _mod_kernel(c_ref, w_ref, b_ref, o_ref):
    c_hi, c_mid, _ = _split3(c_ref[...])
    w_hi, w_mid, _ = _split3(w_ref[...])
    acc = _dot(c_hi, w_hi) + _dot(c_hi, w_mid) + _dot(c_mid, w_hi)
    o_ref[...] = acc + b_ref[...]


def _mod_call(c, w_mod, b_mod):
    bsz, d = c.shape
    n = w_mod.shape[1]
    bn = 1024
    return pl.pallas_call(
        _mod_kernel,
        grid=(n // bn,),
        in_specs=[pl.BlockSpec((bsz, d), lambda j: (0, 0)),
                  pl.BlockSpec((d, bn), lambda j: (0, j)),
                  pl.BlockSpec((1, bn), lambda j: (0, j))],
        out_specs=pl.BlockSpec((bsz, bn), lambda j: (0, j)),
        out_shape=jax.ShapeDtypeStruct((bsz, n), F32),
        compiler_params=_params(("parallel",)),
        name="mod",
    )(c, w_mod, b_mod.reshape(1, n))


def _inproj_kernel(x_ref, mod_ref, g_ref, w_ref, wt_ref, wf_ref, bf_ref, sel_ref,
                   u_ref, ka_ref, ga_ref, gb_ref, qt_ref, vt_ref,
                   carry_ref, *, tiles_per_seq, d_model):
    i = pl.program_id(0)
    tm = x_ref.shape[0]
    h = _rms_mod(x_ref[...], g_ref[...], mod_ref[0, 0:1, :], mod_ref[0, 1:2, :])
    hb = h.astype(BF16)

    cw = 512
    n_c = d_model // cw

    def proj(c0):
        return _dot(hb, w_ref[:, c0:c0 + cw])

    u_ref[...] = proj(0).astype(BF16)
    base = POOL_W
    pairs_per_chunk = cw // LANES
    for c in range(n_c):
        kc = proj(base + c * cw).astype(BF16)
        for pp in range(pairs_per_chunk):
            p = c * pairs_per_chunk + pp
            ka_ref[:, p * 2 * LANES:p * 2 * LANES + LANES] = kc[:, pp * LANES:(pp + 1) * LANES]
    base += d_model
    for c in range(n_c):
        ga_ref[:, c * cw:(c + 1) * cw] = jax.nn.sigmoid(proj(base + c * cw)).astype(BF16)
    base += d_model
    for c in range(n_c):
        gb_ref[:, c * cw:(c + 1) * cw] = jax.nn.sigmoid(proj(base + c * cw)).astype(BF16)

    scale = HEAD_DIM ** -0.5 * LOG2_E
    for c in range(n_c):
        qt_ref[c * cw:(c + 1) * cw, :] = (_dot_nt(wt_ref[c * cw:(c + 1) * cw, :], hb) * scale).astype(BF16)
    for c in range(n_c):
        r0 = d_model + c * cw
        vt_ref[c * cw:(c + 1) * cw, :] = _dot_nt(wt_ref[r0:r0 + cw, :], hb).astype(BF16)

    zf = _dot(hb, wf_ref[...]) + bf_ref[...]
    logf = jnp.minimum(zf, 0.0) - jnp.log1p(jnp.exp(-jnp.abs(zf)))
    row = lax.broadcasted_iota(jnp.int32, (tm, tm), 0)
    col = lax.broadcasted_iota(jnp.int32, (tm, tm), 1)
    tril = jnp.where(row >= col, 1.0, 0.0).astype(BF16)
    hi, mid, lo = _split3(logf)
    cs = _dot(tril, hi) + _dot(tril, mid) + _dot(tril, lo)

    @pl.when(i % tiles_per_seq == 0)
    def _():
        carry_ref[...] = jnp.zeros_like(carry_ref)

    f_cum = cs + carry_ref[0:1, :]
    carry_ref[...] = jnp.broadcast_to(f_cum[tm - 1:tm, :], carry_ref.shape)
    n_hi, n_mid, n_lo = _split3(f_cum * -LOG2_E)
    faug = _dot(n_hi, sel_ref[0]) + _dot(n_mid, sel_ref[1]) + _dot(n_lo, sel_ref[2])
    for p in range(d_model // LANES):
        ka_ref[:, p * 2 * LANES + LANES:(p + 1) * 2 * LANES] = faug[:, p * LANES:(p + 1) * LANES].astype(BF16)


def _inproj_call(x2d, mod3, norm_g, w_tok, w_feat_t, w_f, b_f, sel, seq):
    t, d = x2d.shape
    tm = INPROJ_TM
    tps = seq // tm
    row = lambda i: (i, 0)
    colb = lambda i: (0, i)
    c2 = lambda i: (0, 0)
    out_shape = (
        jax.ShapeDtypeStruct((t, POOL_W), BF16),
        jax.ShapeDtypeStruct((t, 2 * d), BF16),
        jax.ShapeDtypeStruct((t, d), BF16), jax.ShapeDtypeStruct((t, d), BF16),
        jax.ShapeDtypeStruct((d, t), BF16), jax.ShapeDtypeStruct((d, t), BF16),
    )
    out_specs = (
        pl.BlockSpec((tm, POOL_W), row),
        pl.BlockSpec((tm, 2 * d), row),
        pl.BlockSpec((tm, d), row), pl.BlockSpec((tm, d), row),
        pl.BlockSpec((d, tm), colb), pl.BlockSpec((d, tm), colb),
    )
    return pl.pallas_call(
        functools.partial(_inproj_kernel, tiles_per_seq=tps, d_model=d),
        grid=(t // tm,),
        in_specs=[pl.BlockSpec((tm, d), row),
                  pl.BlockSpec((1, N_MOD, d), lambda i: (i // tps, 0, 0)),
                  pl.BlockSpec((1, d), c2),
                  pl.BlockSpec(w_tok.shape, c2),
                  pl.BlockSpec(w_feat_t.shape, c2),
                  pl.BlockSpec(w_f.shape, c2),
                  pl.BlockSpec(b_f.shape, c2),
                  pl.BlockSpec(sel.shape, lambda i: (0, 0, 0))],
        out_specs=out_specs,
        out_shape=out_shape,
        scratch_shapes=[pltpu.VMEM((8, LANES), F32)],
        compiler_params=_params(("arbitrary",)),
        name="inproj",
    )(x2d, mod3, norm_g, w_tok, w_feat_t, w_f, b_f, sel)


def _forget_selectors():
    h = jnp.arange(LANES)[:, None]
    c = jnp.arange(ATTN_HEADS // 2 * LANES)[None, :]
    sels = []
    for r in range(3):
        hit = (h < ATTN_HEADS) & (c == (h // 2) * LANES + 3 * (h % 2) + r)
        sels.append(jnp.where(hit, 1.0, 0.0))
    return jnp.stack(sels).astype(BF16)


SUM_ROWS = 16


def _attn_kernel(qt_ref, ka_ref, vt_ref, o_ref, *, tq, tk):
    seq = ka_ref.shape[0]
    rowid = lax.broadcasted_iota(jnp.int32, (LANES, tq), 0)
    krow = lax.broadcasted_iota(jnp.int32, (tk, tq), 0)
    qcol = lax.broadcasted_iota(jnp.int32, (tk, tq), 1)
    aug = [jnp.where((rowid >= 3 * h) & (rowid < 3 * h + 3), 1.0, 0.0).astype(BF16) for h in range(2)]
    head = [jnp.where(rowid // HEAD_DIM == h, 1.0, 0.0).astype(BF16) for h in range(2)]
    ones_rows = jnp.ones((SUM_ROWS, tk), BF16)

    n_heads = qt_ref.shape[0] // HEAD_DIM

    def q_tile(qi, carry):
        q0 = pl.multiple_of(qi * tq, tq)
        ws = []
        for hh in range(n_heads):
            pair, h = hh // 2, hh % 2
            qt = qt_ref[pair * LANES:(pair + 1) * LANES, pl.ds(q0, tq)]
            ws.append(jnp.concatenate([qt * head[h], aug[h]], axis=0))

        def qk(j):
            k0 = pl.multiple_of(j * tk, tk)
            out = []
            for hh in range(n_heads):
                pair = hh // 2
                ka = ka_ref[pl.ds(k0, tk), pair * 2 * LANES:(pair + 1) * 2 * LANES]
                out.append(_dot(ka, ws[hh]))
            return tuple(out)

        def kv_step(j, state, scores, diag):
            k0 = pl.multiple_of(j * tk, tk)
            new = []
            for h in range(n_heads):
                m, acc = state[h]
                s = scores[h]
                if diag is not None:
                    s = jnp.where(krow + diag * tk <= qcol, s, NEG_INF)
                m_new = jnp.maximum(m, jnp.max(s, axis=0, keepdims=True))
                alpha = jnp.exp2(m - m_new)
                p = jnp.exp2(s - m_new)
                vt = vt_ref[h * HEAD_DIM:(h + 1) * HEAD_DIM, pl.ds(k0, tk)]
                acc = alpha * acc + _dot(jnp.concatenate([vt, ones_rows], axis=0), p.astype(BF16))
                new.append((m_new, acc))
            return tuple(new)

        init = tuple((jnp.full((1, tq), NEG_INF, F32), jnp.zeros((HEAD_DIM + SUM_ROWS, tq), F32))
                     for _ in range(n_heads))
        state = lax.fori_loop(0, qi, lambda j, st: kv_step(j, st, qk(j), None), init)
        state = kv_step(qi, state, qk(qi), 0)
        out_t = jnp.concatenate([st[1][:HEAD_DIM] / st[1][HEAD_DIM:HEAD_DIM + 1] for st in state], axis=0)
        o_ref[pl.ds(q0, tq), :] = out_t.T.astype(o_ref.dtype)
        return carry

    lax.fori_loop(0, seq // tq, q_tile, 0)


def _attn_call(qt, ka, vt, bsz, seq):
    d, t = qt.shape
    blk = ATTN_PAIRS * LANES
    return pl.pallas_call(
        functools.partial(_attn_kernel, tq=ATTN_TQ, tk=ATTN_TK),
        grid=(bsz, d // blk),
        in_specs=[pl.BlockSpec((blk, seq), lambda b, p: (p, b)),
                  pl.BlockSpec((seq, 2 * blk), lambda b, p: (b, p)),
                  pl.BlockSpec((blk, seq), lambda b, p: (p, b))],
        out_specs=pl.BlockSpec((seq, blk), lambda b, p: (b, p)),
        out_shape=jax.ShapeDtypeStruct((t, d), BF16),
        compiler_params=_params(("parallel", "parallel")),
        name="attn",
    )(qt, ka, vt)


def _mix_kernel(uc_ref, up_ref, ga_ref, gb_ref, yb_ref, x_ref, mod_ref, wp_ref, ps_ref,
                wo_ref, g2_ref, wq_ref, sk_ref,
                x1_ref, h2_ref, sc_ref, y_ref, *, tiles_per_seq):
    i = pl.program_id(0)
    tm = x_ref.shape[0]
    halo = LANES
    first = (i % tiles_per_seq) == 0
    pos0 = (i % tiles_per_seq) * tm
    r_d = lax.broadcasted_iota(jnp.int32, (tm, tm), 0)
    c_d = lax.broadcasted_iota(jnp.int32, (tm, tm), 1)
    r_o = lax.broadcasted_iota(jnp.int32, (tm, halo), 0)
    c_o = lax.broadcasted_iota(jnp.int32, (tm, halo), 1)
    pos = pos0 + lax.broadcasted_iota(jnp.int32, (tm, 1), 0)
    gw = wp_ref.shape[2]
    for g, w in enumerate(POOL_WINDOWS):
        lo, hi = g * POOL_GROUP_W, (g + 1) * POOL_GROUP_W
        u_cur = uc_ref[:, lo:hi]
        u_prev = up_ref[tm - halo:tm, lo:hi]
        lag = r_d - c_d
        band_d = jnp.where((lag >= 0) & (lag < w), 1.0, 0.0).astype(BF16)
        band_o = jnp.where((r_o + halo - c_o < w) & jnp.logical_not(first), 1.0, 0.0).astype(BF16)
        wsum = _dot(band_d, u_cur) + _dot(band_o, u_prev)
        cnt = jnp.minimum(pos + 1, w).astype(F32)
        pooled = wsum / cnt - u_cur.astype(F32)
        ya = _dot(pooled.astype(BF16), wp_ref[g]) * ps_ref[:, g * gw:(g + 1) * gw]
        sl = slice(g * gw, (g + 1) * gw)
        y = ga_ref[:, sl].astype(F32) * ya + gb_ref[:, sl].astype(F32) * yb_ref[:, sl].astype(F32)
        y_ref[:, sl] = y.astype(BF16)

    o = _dot(y_ref[...], wo_ref[...])
    x1 = x_ref[...] + mod_ref[0, 2:3, :] * o
    x1_ref[...] = x1
    h2f = _rms_mod(x1, g2_ref[...], mod_ref[0, 3:4, :], mod_ref[0, 4:5, :])
    h2 = h2f.astype(BF16)
    h2_ref[...] = h2f.T.astype(BF16)
    n_hp = sk_ref.shape[0]
    half = sk_ref.shape[2]
    qw = 512
    per = qw // half
    for c in range(n_hp // per):
        qp = _dot(h2, wq_ref[:, c * qw:(c + 1) * qw]).astype(BF16)
        for k in range(per):
            hp = c * per + k
            sc_ref[hp] = _dot_nt(sk_ref[hp], qp[:, k * half:(k + 1) * half])


def _mix_call(u, ga, gb, yb, x2d, mod3, w_pool, pool_scale, w_out, norm2_g, w_query, sub_keys, seq):
    t, d = x2d.shape
    tm = MIX_TM
    tps = seq // tm
    row = lambda i: (i, 0)
    c2 = lambda i: (0, 0)
    c3 = lambda i: (0, 0, 0)
    n_hp = sub_keys.shape[0]
    return pl.pallas_call(
        functools.partial(_mix_kernel, tiles_per_seq=tps),
        grid=(t // tm,),
        in_specs=[pl.BlockSpec((tm, POOL_W), row),
                  pl.BlockSpec((tm, POOL_W), lambda i: (jnp.maximum(i - 1, 0), 0)),
                  pl.BlockSpec((tm, d), row), pl.BlockSpec((tm, d), row), pl.BlockSpec((tm, d), row),
                  pl.BlockSpec((tm, d), row),
                  pl.BlockSpec((1, N_MOD, d), lambda i: (i // tps, 0, 0)),
                  pl.BlockSpec(w_pool.shape, c3),
                  pl.BlockSpec((1, d), c2),
                  pl.BlockSpec((d, d), c2),
                  pl.BlockSpec((1, d), c2),
                  pl.BlockSpec(w_query.shape, c2),
                  pl.BlockSpec(sub_keys.shape, c3)],
        out_specs=(pl.BlockSpec((tm, d), row), pl.BlockSpec((d, tm), lambda i: (0, i)),
                   pl.BlockSpec((n_hp, N_KEYS, tm), lambda i: (0, 0, i))),
        out_shape=(jax.ShapeDtypeStruct((t, d), F32), jax.ShapeDtypeStruct((d, t), BF16),
                   jax.ShapeDtypeStruct((n_hp, N_KEYS, t), F32)),
        scratch_shapes=[pltpu.VMEM((tm, d), BF16)],
        compiler_params=_params(("parallel",)),
        name="mix",
    )(u, u, ga, gb, yb, x2d, mod3, w_pool, pool_scale, w_out, norm2_g, w_query, sub_keys)


RANK_CODE_BASE = -(2.0 ** 127)


def _top16_rows(s):
    cur = s
    rows = []
    for r in range(PEER_TOPK):
        m = jnp.max(cur, axis=0, keepdims=True)
        rows.append(m)
        cur = jnp.where(cur == m, RANK_CODE_BASE * (1.0 + r / 64.0), cur)
    bits = lax.bitcast_convert_type(cur, jnp.int32)
    coded = ((bits >> 17) & 63).astype(F32)
    rank = jnp.where(cur <= RANK_CODE_BASE, coded, float(PEER_TOPK))
    return rows, rank


def _stack_rows(rows, tm):
    n = len(rows)
    rid = lax.broadcasted_iota(jnp.int32, (n, tm), 0)
    arr = jnp.zeros((n, tm), F32)
    for r, v in enumerate(rows):
        arr = jnp.where(rid == r, v, arr)
    return arr


def _candidates(rows1, arr1_hi, rows2, arr2, combine):
    pieces = [combine(rows1[0], arr2)]
    for a in range(1, 8):
        pieces.append(combine(rows1[a], arr2[0:8]))
    pieces.append(combine(arr1_hi, rows2[0]))
    return jnp.concatenate(pieces, axis=0)


def _dup_bf16(x):
    bits = lax.bitcast_convert_type(x.astype(BF16).astype(F32), jnp.uint32)
    return bits | (bits >> 16)


def _row_tile_bf16(ref, h, row, tm):
    words = jnp.broadcast_to(ref[h, pl.ds(row, 1), :], (8, tm))
    return pltpu.bitcast(words, BF16)


def _peer_stats(sc_ref, e1_ref, lim_ref, e2_ref, r2_ref, h):
    tm = sc_ref.shape[2]
    s1 = sc_ref[2 * h]
    s2 = sc_ref[2 * h + 1]
    t1, rank1 = _top16_rows(s1)
    t2, rank2 = _top16_rows(s2)
    t1_hi = _stack_rows(t1[8:], tm)
    t2_arr = _stack_rows(t2, tm)
    cand = _candidates(t1, t1_hi, t2, t2_arr, lambda a, b: a + b)
    n_c = cand.shape[0]
    cid = lax.broadcasted_iota(jnp.int32, (n_c, tm), 0)
    cur = cand
    z = jnp.zeros((1, tm), F32)
    m0 = None
    for r in range(PEER_TOPK):
        m = jnp.max(cur, axis=0, keepdims=True)
        if r == 0:
            m0 = m
        first = jnp.min(jnp.where(cur == m, cid, n_c), axis=0, keepdims=True)
        cur = jnp.where(cid == first, -jnp.inf, cur)
        z = z + jnp.exp(m - m0)
    sel = jnp.where(cur == -jnp.inf, 1.0, 0.0)
    width = [jnp.sum(sel[0:16], axis=0, keepdims=True)]
    for a in range(1, 8):
        width.append(jnp.sum(sel[8 + 8 * a:16 + 8 * a], axis=0, keepdims=True))
    for a in range(8, PEER_TOPK):
        width.append(sel[64 + a:65 + a])
    lim = jnp.zeros_like(s1)
    for a in range(PEER_TOPK):
        lim = jnp.where(rank1 == float(a), width[a], lim)
    inv_z = 1.0 / z
    e1_ref[h] = _dup_bf16(jnp.exp(s1 - t1[0]) * inv_z)
    lim_ref[h] = _dup_bf16(lim)
    e2_ref[h] = jnp.exp(s2 - t2[0]).astype(BF16)
    r2_ref[h] = rank2.astype(BF16)


def _peer_kernel(sc_ref, h2_ref, u_ref, vt_ref, x1_ref, mod_ref, fg_ref, o_ref,
                 e1_ref, lim_ref, e2_ref, r2_ref, acc_ref, y_ref):
    s = pl.program_id(1)
    n_heads = e1_ref.shape[0]
    eb = u_ref.shape[0]
    tm = h2_ref.shape[1]
    rows_per_blk = eb // N_KEYS
    kc = 2 * N_KEYS

    @pl.when(s == 0)
    def _():
        def head(h, carry):
            _peer_stats(sc_ref, e1_ref, lim_ref, e2_ref, r2_ref, h)
            return carry
        lax.fori_loop(0, n_heads, head, 0)
        acc_ref[...] = jnp.zeros_like(acc_ref)

    def expert_block():
        a_chunks = [_dot(u_ref[c * kc:(c + 1) * kc, :], h2_ref[...]) for c in range(eb // kc)]
        for ii in range(rows_per_blk):
            i_key = s * rows_per_blk + ii
            sub = 16
            g = None
            for h in range(n_heads):
                p = _row_tile_bf16(e1_ref, h, i_key, tm)[None] * e2_ref[h].reshape(N_KEYS // sub, sub, tm)
                lim = _row_tile_bf16(lim_ref, h, i_key, tm)[None]
                term = jnp.where(r2_ref[h].reshape(N_KEYS // sub, sub, tm) < lim, p, jnp.zeros_like(p))
                g = term if g is None else g + term
            g = g.reshape(N_KEYS, tm)
            r_in = ii * N_KEYS % kc
            a = a_chunks[ii * N_KEYS // kc][r_in:r_in + N_KEYS].astype(BF16)
            act = (0.5 * a) * (1.0 + lax.erf(a * (0.5 ** 0.5)))
            y_ref[ii * N_KEYS:(ii + 1) * N_KEYS, :] = act * g
            if (ii + 1) * N_KEYS % kc == 0:
                c0 = (ii + 1) * N_KEYS - kc
                acc_ref[...] += _dot(vt_ref[:, c0:c0 + kc], y_ref[c0:c0 + kc, :])

    expert_block()

    @pl.when(s == pl.num_programs(1) - 1)
    def _():
        peer = acc_ref[...].T
        x2 = x1_ref[...] + mod_ref[0, 5:6, :] * peer
        ms = jnp.mean(x2 * x2, axis=-1, keepdims=True)
        o_ref[...] = x2 * lax.rsqrt(ms + EPS) * fg_ref[...]


def _peer_call(sc, h2, u_bf, vt_bf, x1, mod3, final_g, seq):
    t, d = x1.shape
    tm, eb = PEER_TM, PEER_EB
    tps = seq // tm
    n_hp = sc.shape[0]
    n_exp = u_bf.shape[0]
    row = lambda i, j: (i, 0)
    return pl.pallas_call(
        _peer_kernel,
        grid=(t // tm, n_exp // eb),
        in_specs=[pl.BlockSpec((n_hp, N_KEYS, tm), lambda i, j: (0, 0, i)),
                  pl.BlockSpec((d, tm), lambda i, j: (0, i)),
                  pl.BlockSpec((eb, d), lambda i, j: (j, 0)),
                  pl.BlockSpec((d, eb), lambda i, j: (0, j)),
                  pl.BlockSpec((tm, d), row),
                  pl.BlockSpec((1, N_MOD, d), lambda i, j: (i // tps, 0, 0)),
                  pl.BlockSpec((1, d), lambda i, j: (0, 0))],
        out_specs=pl.BlockSpec((tm, d), row),
        out_shape=jax.ShapeDtypeStruct((t, d), F32),
        scratch_shapes=[pltpu.VMEM((PEER_HEADS, N_KEYS, tm), jnp.uint32),
                        pltpu.VMEM((PEER_HEADS, N_KEYS, tm), jnp.uint32),
                        pltpu.VMEM((PEER_HEADS, N_KEYS, tm), BF16),
                        pltpu.VMEM((PEER_HEADS, N_KEYS, tm), BF16),
                        pltpu.VMEM((d, tm), F32),
                        pltpu.VMEM((eb, tm), BF16)],
        compiler_params=_params(("parallel", "arbitrary")),
        name="peer",
    )(sc, h2, u_bf, vt_bf, x1, mod3, final_g)


def kernel(x, c, w_mod, b_mod, norm1_g, w_in, b_f, w_pool, pool_scale, w_out, norm2_g,
           peer_w_query, peer_sub_keys, peer_u, peer_v, final_g):
    bsz, seq, d = x.shape
    depth = w_mod.shape[0]
    t = bsz * seq
    assert d == ATTN_HEADS * HEAD_DIM and seq % INPROJ_TM == 0 and t % PEER_TM == 0
    x2d = x.reshape(t, d)
    out = x2d
    for l in range(depth):
        mod3 = _mod_call(c, w_mod[l], b_mod[l]).reshape(bsz, N_MOD, d)

        w = w_in[l]
        o_q, o_k, o_v = POOL_W, POOL_W + d, POOL_W + 2 * d
        o_f = POOL_W + 3 * d
        o_g = o_f + ATTN_HEADS
        w_tok = jnp.concatenate([w[:, :o_q], w[:, o_k:o_v], w[:, o_g:]], axis=1).astype(BF16)
        w_feat_t = jnp.concatenate([w[:, o_q:o_k], w[:, o_v:o_f]], axis=1).T.astype(BF16)
        w_f = jnp.pad(w[:, o_f:o_g], ((0, 0), (0, LANES - ATTN_HEADS))).astype(BF16)
        b_f_row = jnp.pad(b_f[l], (0, LANES - ATTN_HEADS)).reshape(1, LANES)
        u, ka, ga, gb, qt, vt = _inproj_call(
            x2d, mod3, norm1_g[l].reshape(1, d), w_tok, w_feat_t, w_f, b_f_row,
            _forget_selectors(), seq)

        yb = _attn_call(qt, ka, vt, bsz, seq)

        n_hp = 2 * PEER_HEADS
        x1, h2, sc = _mix_call(
            u, ga, gb, yb, x2d, mod3, w_pool[l].astype(BF16), pool_scale[l].reshape(1, d),
            w_out[l].astype(BF16), norm2_g[l].reshape(1, d), peer_w_query[l].astype(BF16),
            peer_sub_keys[l].reshape(n_hp, N_KEYS, -1).astype(BF16), seq)

        assert l == depth - 1 == 0
        out = _peer_call(sc, h2, peer_u[l].astype(BF16), peer_v[l].T.astype(BF16),
                         x1, mod3, final_g.reshape(1, d), seq)
        x2d = out
    return out.reshape(bsz, seq, d)
```

```python
import functools

import jax
import jax.numpy as jnp
from jax import lax
from jax.experimental import pallas as pl
from jax.experimental.pallas import tpu as pltpu

F32 = jnp.float32
BF16 = jnp.bfloat16

EPS = 1e-6
NEG_INF = -1e30
LOG2_E = 1.4426950408889634

POOL_WINDOWS = (2, 4, 8, 16)
POOL_GROUP_W = 128
POOL_W = len(POOL_WINDOWS) * POOL_GROUP_W
ATTN_HEADS = 16
HEAD_DIM = 64
PEER_HEADS = 8
N_KEYS = 128
PEER_TOPK = 16
N_MOD = 6

LANES = 128
BF16_ROWS = 16
MXU_COLS = 256
MATMUL_N = 2 * MXU_COLS
VMEM_LIMIT = 56 * 1024 * 1024

INPROJ_TM = 512
ATTN_TQ = 512
ATTN_TK = 512
ATTN_PAIRS = 4
MIX_TM = 512
PEER_TM = 512
PEER_EB = 2048


def _params(sem):
    return pltpu.CompilerParams(dimension_semantics=sem, vmem_limit_bytes=VMEM_LIMIT)


def _split3(a):
    hi = a.astype(BF16)
    r1 = a - hi.astype(F32)
    mid = r1.astype(BF16)
    lo = (r1 - mid.astype(F32)).astype(BF16)
    return hi, mid, lo


def _dot(a, b):
    return jnp.dot(a, b, preferred_element_type=F32)


def _dot_nt(a, b):
    return lax.dot_general(a, b, (((1,), (1,)), ((), ())), preferred_element_type=F32)


def _rms_mod(x, g, shift, scale):
    ms = jnp.mean(x * x, axis=-1, keepdims=True)
    y = x * lax.rsqrt(ms + EPS) * g
    return y * (1.0 + scale) + shift


def _mod_kernel(c_ref, w_ref, b_ref, o_ref):
    c_hi, c_mid, _ = _split3(c_ref[...])
    w_hi, w_mid, _ = _split3(w_ref[...])
    acc = _dot(c_hi, w_hi) + _dot(c_hi, w_mid) + _dot(c_mid, w_hi)
    o_ref[...] = acc + b_ref[...]


def _mod_call(c, w_mod, b_mod):
    bsz, d = c.shape
    n = w_mod.shape[1]
    bn = 1024
    return pl.pallas_call(
        _mod_kernel,
        grid=(n // bn,),
        in_specs=[pl.BlockSpec((bsz, d), lambda j: (0, 0)),
                  pl.BlockSpec((d, bn), lambda j: (0, j)),
                  pl.BlockSpec((1, bn), lambda j: (0, j))],
        out_specs=pl.BlockSpec((bsz, bn), lambda j: (0, j)),
        out_shape=jax.ShapeDtypeStruct((bsz, n), F32),
        compiler_params=_params(("parallel",)),
        name="mod",
    )(c, w_mod, b_mod.reshape(1, n))


def _inproj_kernel(x_ref, mod_ref, g_ref, w_ref, wt_ref, wf_ref, bf_ref, sel_ref,
                   u_ref, ka_ref, ga_ref, gb_ref, qt_ref, vt_ref,
                   carry_ref, *, tiles_per_seq, d_model):
    i = pl.program_id(0)
    tm = x_ref.shape[0]
    h = _rms_mod(x_ref[...], g_ref[...], mod_ref[0, 0:1, :], mod_ref[0, 1:2, :])
    hb = h.astype(BF16)

    cw = MATMUL_N
    n_c = d_model // cw

    def proj(c0):
        return _dot(hb, w_ref[:, c0:c0 + cw])

    u_ref[...] = proj(0).astype(BF16)
    base = POOL_W
    pairs_per_chunk = cw // LANES
    for c in range(n_c):
        kc = proj(base + c * cw).astype(BF16)
        for pp in range(pairs_per_chunk):
            p = c * pairs_per_chunk + pp
            ka_ref[:, p * 2 * LANES:p * 2 * LANES + LANES] = kc[:, pp * LANES:(pp + 1) * LANES]
    base += d_model
    for c in range(n_c):
        ga_ref[:, c * cw:(c + 1) * cw] = jax.nn.sigmoid(proj(base + c * cw)).astype(BF16)
    base += d_model
    for c in range(n_c):
        gb_ref[:, c * cw:(c + 1) * cw] = jax.nn.sigmoid(proj(base + c * cw)).astype(BF16)

    scale = HEAD_DIM ** -0.5 * LOG2_E
    for c in range(n_c):
        qt_ref[c * cw:(c + 1) * cw, :] = (_dot_nt(wt_ref[c * cw:(c + 1) * cw, :], hb) * scale).astype(BF16)
    for c in range(n_c):
        r0 = d_model + c * cw
        vt_ref[c * cw:(c + 1) * cw, :] = _dot_nt(wt_ref[r0:r0 + cw, :], hb).astype(BF16)

    zf = _dot(hb, wf_ref[...]) + bf_ref[...]
    logf = jnp.minimum(zf, 0.0) - jnp.log1p(jnp.exp(-jnp.abs(zf)))
    row = lax.broadcasted_iota(jnp.int32, (tm, tm), 0)
    col = lax.broadcasted_iota(jnp.int32, (tm, tm), 1)
    tril = jnp.where(row >= col, 1.0, 0.0).astype(BF16)
    hi, mid, lo = _split3(logf)
    cs = _dot(tril, hi) + _dot(tril, mid) + _dot(tril, lo)

    @pl.when(i % tiles_per_seq == 0)
    def _():
        carry_ref[...] = jnp.zeros_like(carry_ref)

    f_cum = cs + carry_ref[0:1, :]
    carry_ref[...] = jnp.broadcast_to(f_cum[tm - 1:tm, :], carry_ref.shape)
    n_hi, n_mid, n_lo = _split3(f_cum * -LOG2_E)
    faug = _dot(n_hi, sel_ref[0]) + _dot(n_mid, sel_ref[1]) + _dot(n_lo, sel_ref[2])
    for p in range(d_model // LANES):
        ka_ref[:, p * 2 * LANES + LANES:(p + 1) * 2 * LANES] = faug[:, p * LANES:(p + 1) * LANES].astype(BF16)


def _inproj_call(x2d, mod3, norm_g, w_tok, w_feat_t, w_f, b_f, sel, seq):
    t, d = x2d.shape
    tm = INPROJ_TM
    tps = seq // tm
    row = lambda i: (i, 0)
    colb = lambda i: (0, i)
    c2 = lambda i: (0, 0)
    out_shape = (
        jax.ShapeDtypeStruct((t, POOL_W), BF16),
        jax.ShapeDtypeStruct((t, 2 * d), BF16),
        jax.ShapeDtypeStruct((t, d), BF16), jax.ShapeDtypeStruct((t, d), BF16),
        jax.ShapeDtypeStruct((d, t), BF16), jax.ShapeDtypeStruct((d, t), BF16),
    )
    out_specs = (
        pl.BlockSpec((tm, POOL_W), row),
        pl.BlockSpec((tm, 2 * d), row),
        pl.BlockSpec((tm, d), row), pl.BlockSpec((tm, d), row),
        pl.BlockSpec((d, tm), colb), pl.BlockSpec((d, tm), colb),
    )
    return pl.pallas_call(
        functools.partial(_inproj_kernel, tiles_per_seq=tps, d_model=d),
        grid=(t // tm,),
        in_specs=[pl.BlockSpec((tm, d), row),
                  pl.BlockSpec((1, N_MOD, d), lambda i: (i // tps, 0, 0)),
                  pl.BlockSpec((1, d), c2),
                  pl.BlockSpec(w_tok.shape, c2),
                  pl.BlockSpec(w_feat_t.shape, c2),
                  pl.BlockSpec(w_f.shape, c2),
                  pl.BlockSpec(b_f.shape, c2),
                  pl.BlockSpec(sel.shape, lambda i: (0, 0, 0))],
        out_specs=out_specs,
        out_shape=out_shape,
        scratch_shapes=[pltpu.VMEM((8, LANES), F32)],
        compiler_params=_params(("arbitrary",)),
        name="inproj",
    )(x2d, mod3, norm_g, w_tok, w_feat_t, w_f, b_f, sel)


def _forget_selectors():
    h = jnp.arange(LANES)[:, None]
    c = jnp.arange(ATTN_HEADS // 2 * LANES)[None, :]
    sels = []
    for r in range(3):
        hit = (h < ATTN_HEADS) & (c == (h // 2) * LANES + 3 * (h % 2) + r)
        sels.append(jnp.where(hit, 1.0, 0.0))
    return jnp.stack(sels).astype(BF16)


SUM_ROWS = BF16_ROWS


def _attn_kernel(qt_ref, ka_ref, vt_ref, o_ref, *, tq, tk):
    seq = ka_ref.shape[0]
    rowid = lax.broadcasted_iota(jnp.int32, (LANES, tq), 0)
    krow = lax.broadcasted_iota(jnp.int32, (tk, tq), 0)
    qcol = lax.broadcasted_iota(jnp.int32, (tk, tq), 1)
    aug = [jnp.where((rowid >= 3 * h) & (rowid < 3 * h + 3), 1.0, 0.0).astype(BF16) for h in range(2)]
    head = [jnp.where(rowid // HEAD_DIM == h, 1.0, 0.0).astype(BF16) for h in range(2)]
    ones_rows = jnp.ones((SUM_ROWS, tk), BF16)

    n_heads = qt_ref.shape[0] // HEAD_DIM

    def q_tile(qi, carry):
        q0 = pl.multiple_of(qi * tq, tq)
        ws = []
        for hh in range(n_heads):
            pair, h = hh // 2, hh % 2
            qt = qt_ref[pair * LANES:(pair + 1) * LANES, pl.ds(q0, tq)]
            ws.append(jnp.concatenate([qt * head[h], aug[h]], axis=0))

        def qk(j):
            k0 = pl.multiple_of(j * tk, tk)
            out = []
            for hh in range(n_heads):
                pair = hh // 2
                ka = ka_ref[pl.ds(k0, tk), pair * 2 * LANES:(pair + 1) * 2 * LANES]
                out.append(_dot(ka, ws[hh]))
            return tuple(out)

        def kv_step(j, state, scores, diag):
            k0 = pl.multiple_of(j * tk, tk)
            new = []
            for h in range(n_heads):
                m, acc = state[h]
                s = scores[h]
                if diag is not None:
                    s = jnp.where(krow + diag * tk <= qcol, s, NEG_INF)
                m_new = jnp.maximum(m, jnp.max(s, axis=0, keepdims=True))
                alpha = jnp.exp2(m - m_new)
                p = jnp.exp2(s - m_new)
                vt = vt_ref[h * HEAD_DIM:(h + 1) * HEAD_DIM, pl.ds(k0, tk)]
                acc = alpha * acc + _dot(jnp.concatenate([vt, ones_rows], axis=0), p.astype(BF16))
                new.append((m_new, acc))
            return tuple(new)

        init = tuple((jnp.full((1, tq), NEG_INF, F32), jnp.zeros((HEAD_DIM + SUM_ROWS, tq), F32))
                     for _ in range(n_heads))
        state = lax.fori_loop(0, qi, lambda j, st: kv_step(j, st, qk(j), None), init)
        state = kv_step(qi, state, qk(qi), 0)
        out_t = jnp.concatenate([st[1][:HEAD_DIM] / st[1][HEAD_DIM:HEAD_DIM + 1] for st in state], axis=0)
        o_ref[pl.ds(q0, tq), :] = out_t.T.astype(o_ref.dtype)
        return carry

    lax.fori_loop(0, seq // tq, q_tile, 0)


def _attn_call(qt, ka, vt, bsz, seq):
    d, t = qt.shape
    blk = ATTN_PAIRS * LANES
    return pl.pallas_call(
        functools.partial(_attn_kernel, tq=ATTN_TQ, tk=ATTN_TK),
        grid=(bsz, d // blk),
        in_specs=[pl.BlockSpec((blk, seq), lambda b, p: (p, b)),
                  pl.BlockSpec((seq, 2 * blk), lambda b, p: (b, p)),
                  pl.BlockSpec((blk, seq), lambda b, p: (p, b))],
        out_specs=pl.BlockSpec((seq, blk), lambda b, p: (b, p)),
        out_shape=jax.ShapeDtypeStruct((t, d), BF16),
        compiler_params=_params(("parallel", "parallel")),
        name="attn",
    )(qt, ka, vt)


def _mix_kernel(uc_ref, up_ref, ga_ref, gb_ref, yb_ref, x_ref, mod_ref, wp_ref, ps_ref,
                wo_ref, g2_ref, wq_ref, sk_ref,
                x1_ref, h2_ref, sc_ref, y_ref, *, tiles_per_seq):
    i = pl.program_id(0)
    tm = x_ref.shape[0]
    halo = LANES
    first = (i % tiles_per_seq) == 0
    pos0 = (i % tiles_per_seq) * tm
    r_d = lax.broadcasted_iota(jnp.int32, (tm, tm), 0)
    c_d = lax.broadcasted_iota(jnp.int32, (tm, tm), 1)
    r_o = lax.broadcasted_iota(jnp.int32, (tm, halo), 0)
    c_o = lax.broadcasted_iota(jnp.int32, (tm, halo), 1)
    pos = pos0 + lax.broadcasted_iota(jnp.int32, (tm, 1), 0)
    gw = wp_ref.shape[2]
    for g, w in enumerate(POOL_WINDOWS):
        lo, hi = g * POOL_GROUP_W, (g + 1) * POOL_GROUP_W
        u_cur = uc_ref[:, lo:hi]
        u_prev = up_ref[tm - halo:tm, lo:hi]
        lag = r_d - c_d
        band_d = jnp.where((lag >= 0) & (lag < w), 1.0, 0.0).astype(BF16)
        band_o = jnp.where((r_o + halo - c_o < w) & jnp.logical_not(first), 1.0, 0.0).astype(BF16)
        wsum = _dot(band_d, u_cur) + _dot(band_o, u_prev)
        cnt = jnp.minimum(pos + 1, w).astype(F32)
        pooled = wsum / cnt - u_cur.astype(F32)
        ya = _dot(pooled.astype(BF16), wp_ref[g]) * ps_ref[:, g * gw:(g + 1) * gw]
        sl = slice(g * gw, (g + 1) * gw)
        y = ga_ref[:, sl].astype(F32) * ya + gb_ref[:, sl].astype(F32) * yb_ref[:, sl].astype(F32)
        y_ref[:, sl] = y.astype(BF16)

    o = _dot(y_ref[...], wo_ref[...])
    x1 = x_ref[...] + mod_ref[0, 2:3, :] * o
    x1_ref[...] = x1
    h2f = _rms_mod(x1, g2_ref[...], mod_ref[0, 3:4, :], mod_ref[0, 4:5, :])
    h2 = h2f.astype(BF16)
    h2_ref[...] = h2f.T.astype(BF16)
    n_hp = sk_ref.shape[0]
    half = sk_ref.shape[2]
    qw = MATMUL_N
    per = qw // half
    for c in range(n_hp // per):
        qp = _dot(h2, wq_ref[:, c * qw:(c + 1) * qw]).astype(BF16)
        for k in range(per):
            hp = c * per + k
            sc_ref[hp] = _dot_nt(sk_ref[hp], qp[:, k * half:(k + 1) * half])


def _mix_call(u, ga, gb, yb, x2d, mod3, w_pool, pool_scale, w_out, norm2_g, w_query, sub_keys, seq):
    t, d = x2d.shape
    tm = MIX_TM
    tps = seq // tm
    row = lambda i: (i, 0)
    c2 = lambda i: (0, 0)
    c3 = lambda i: (0, 0, 0)
    n_hp = sub_keys.shape[0]
    return pl.pallas_call(
        functools.partial(_mix_kernel, tiles_per_seq=tps),
        grid=(t // tm,),
        in_specs=[pl.BlockSpec((tm, POOL_W), row),
                  pl.BlockSpec((tm, POOL_W), lambda i: (jnp.maximum(i - 1, 0), 0)),
                  pl.BlockSpec((tm, d), row), pl.BlockSpec((tm, d), row), pl.BlockSpec((tm, d), row),
                  pl.BlockSpec((tm, d), row),
                  pl.BlockSpec((1, N_MOD, d), lambda i: (i // tps, 0, 0)),
                  pl.BlockSpec(w_pool.shape, c3),
                  pl.BlockSpec((1, d), c2),
                  pl.BlockSpec((d, d), c2),
                  pl.BlockSpec((1, d), c2),
                  pl.BlockSpec(w_query.shape, c2),
                  pl.BlockSpec(sub_keys.shape, c3)],
        out_specs=(pl.BlockSpec((tm, d), row), pl.BlockSpec((d, tm), lambda i: (0, i)),
                   pl.BlockSpec((n_hp, N_KEYS, tm), lambda i: (0, 0, i))),
        out_shape=(jax.ShapeDtypeStruct((t, d), F32), jax.ShapeDtypeStruct((d, t), BF16),
                   jax.ShapeDtypeStruct((n_hp, N_KEYS, t), F32)),
        scratch_shapes=[pltpu.VMEM((tm, d), BF16)],
        compiler_params=_params(("parallel",)),
        name="mix",
    )(u, u, ga, gb, yb, x2d, mod3, w_pool, pool_scale, w_out, norm2_g, w_query, sub_keys)


RANK_CODE_BASE = -(2.0 ** 127)


def _select16(s, first_only):
    n = s.shape[0]
    idx = lax.broadcasted_iota(jnp.int32, s.shape, 0) if first_only else None
    cur = s
    rows = []
    for r in range(PEER_TOPK):
        m = jnp.max(cur, axis=0, keepdims=True)
        rows.append(m)
        hit = cur == m
        if first_only:
            hit = idx == jnp.min(jnp.where(hit, idx, n), axis=0, keepdims=True)
        cur = jnp.where(hit, RANK_CODE_BASE * (1.0 + r / 64.0), cur)
    return rows, cur


def _top16_rows(s):
    rows, cur = _select16(s, first_only=False)
    removed = jnp.sum(jnp.where(cur <= RANK_CODE_BASE, 1.0, 0.0), axis=0, keepdims=True)
    tied = jnp.max(removed) > float(PEER_TOPK)
    rows, cur = lax.cond(tied, lambda: _select16(s, first_only=True), lambda: (rows, cur))
    bits = lax.bitcast_convert_type(cur, jnp.int32)
    coded = ((bits >> 17) & 63).astype(F32)
    rank = jnp.where(cur <= RANK_CODE_BASE, coded, float(PEER_TOPK))
    return rows, rank


def _stack_rows(rows, tm):
    n = len(rows)
    rid = lax.broadcasted_iota(jnp.int32, (n, tm), 0)
    arr = jnp.zeros((n, tm), F32)
    for r, v in enumerate(rows):
        arr = jnp.where(rid == r, v, arr)
    return arr


def _candidates(rows1, arr1_hi, rows2, arr2, combine):
    pieces = [combine(rows1[0], arr2)]
    for a in range(1, 8):
        pieces.append(combine(rows1[a], arr2[0:8]))
    pieces.append(combine(arr1_hi, rows2[0]))
    return jnp.concatenate(pieces, axis=0)


def _dup_bf16(x):
    bits = lax.bitcast_convert_type(x.astype(BF16).astype(F32), jnp.uint32)
    return bits | (bits >> 16)


def _row_tile_bf16(ref, h, row, tm):
    words = jnp.broadcast_to(ref[h, pl.ds(row, 1), :], (8, tm))
    return pltpu.bitcast(words, BF16)


def _peer_stats(sc_ref, e1_ref, lim_ref, e2_ref, r2_ref, h):
    tm = sc_ref.shape[2]
    s1 = sc_ref[2 * h]
    s2 = sc_ref[2 * h + 1]
    t1, rank1 = _top16_rows(s1)
    t2, rank2 = _top16_rows(s2)
    t1_hi = _stack_rows(t1[8:], tm)
    t2_arr = _stack_rows(t2, tm)
    cand = _candidates(t1, t1_hi, t2, t2_arr, lambda a, b: a + b)
    n_c = cand.shape[0]
    cid = lax.broadcasted_iota(jnp.int32, (n_c, tm), 0)
    cur = cand
    z = jnp.zeros((1, tm), F32)
    m0 = None
    for r in range(PEER_TOPK):
        m = jnp.max(cur, axis=0, keepdims=True)
        if r == 0:
            m0 = m
        first = jnp.min(jnp.where(cur == m, cid, n_c), axis=0, keepdims=True)
        cur = jnp.where(cid == first, -jnp.inf, cur)
        z = z + jnp.exp(m - m0)
    sel = jnp.where(cur == -jnp.inf, 1.0, 0.0)
    width = [jnp.sum(sel[0:16], axis=0, keepdims=True)]
    for a in range(1, 8):
        width.append(jnp.sum(sel[8 + 8 * a:16 + 8 * a], axis=0, keepdims=True))
    for a in range(8, PEER_TOPK):
        width.append(sel[64 + a:65 + a])
    lim = jnp.zeros_like(s1)
    for a in range(PEER_TOPK):
        lim = jnp.where(rank1 == float(a), width[a], lim)
    inv_z = 1.0 / z
    e1_ref[h] = _dup_bf16(jnp.exp(s1 - t1[0]) * inv_z)
    lim_ref[h] = _dup_bf16(lim)
    e2_ref[h] = jnp.exp(s2 - t2[0]).astype(BF16)
    r2_ref[h] = rank2.astype(BF16)


def _peer_kernel(sc_ref, h2_ref, u_ref, vt_ref, x1_ref, mod_ref, fg_ref, o_ref,
                 e1_ref, lim_ref, e2_ref, r2_ref, acc_ref, y_ref):
    s = pl.program_id(1)
    n_heads = e1_ref.shape[0]
    eb = u_ref.shape[0]
    tm = h2_ref.shape[1]
    rows_per_blk = eb // N_KEYS
    kc = 2 * N_KEYS

    @pl.when(s == 0)
    def _():
        def head(h, carry):
            _peer_stats(sc_ref, e1_ref, lim_ref, e2_ref, r2_ref, h)
            return carry
        lax.fori_loop(0, n_heads, head, 0)
        acc_ref[...] = jnp.zeros_like(acc_ref)

    def expert_block():
        a_chunks = [_dot(u_ref[c * kc:(c + 1) * kc, :], h2_ref[...]) for c in range(eb // kc)]
        for ii in range(rows_per_blk):
            i_key = s * rows_per_blk + ii
            sub = BF16_ROWS
            g = None
            for h in range(n_heads):
                p = _row_tile_bf16(e1_ref, h, i_key, tm)[None] * e2_ref[h].reshape(N_KEYS // sub, sub, tm)
                lim = _row_tile_bf16(lim_ref, h, i_key, tm)[None]
                term = jnp.where(r2_ref[h].reshape(N_KEYS // sub, sub, tm) < lim, p, jnp.zeros_like(p))
                g = term if g is None else g + term
            g = g.reshape(N_KEYS, tm)
            r_in = ii * N_KEYS % kc
            a = a_chunks[ii * N_KEYS // kc][r_in:r_in + N_KEYS].astype(BF16)
            act = (0.5 * a) * (1.0 + lax.erf(a * (0.5 ** 0.5)))
            y_ref[ii * N_KEYS:(ii + 1) * N_KEYS, :] = act * g
            if (ii + 1) * N_KEYS % kc == 0:
                c0 = (ii + 1) * N_KEYS - kc
                acc_ref[...] += _dot(vt_ref[:, c0:c0 + kc], y_ref[c0:c0 + kc, :])

    expert_block()

    @pl.when(s == pl.num_programs(1) - 1)
    def _():
        peer = acc_ref[...].T
        x2 = x1_ref[...] + mod_ref[0, 5:6, :] * peer
        ms = jnp.mean(x2 * x2, axis=-1, keepdims=True)
        o_ref[...] = x2 * lax.rsqrt(ms + EPS) * fg_ref[...]


def _peer_call(sc, h2, u_bf, vt_bf, x1, mod3, final_g, seq):
    t, d = x1.shape
    tm, eb = PEER_TM, PEER_EB
    tps = seq // tm
    n_hp = sc.shape[0]
    n_exp = u_bf.shape[0]
    row = lambda i, j: (i, 0)
    return pl.pallas_call(
        _peer_kernel,
        grid=(t // tm, n_exp // eb),
        in_specs=[pl.BlockSpec((n_hp, N_KEYS, tm), lambda i, j: (0, 0, i)),
                  pl.BlockSpec((d, tm), lambda i, j: (0, i)),
                  pl.BlockSpec((eb, d), lambda i, j: (j, 0)),
                  pl.BlockSpec((d, eb), lambda i, j: (0, j)),
                  pl.BlockSpec((tm, d), row),
                  pl.BlockSpec((1, N_MOD, d), lambda i, j: (i // tps, 0, 0)),
                  pl.BlockSpec((1, d), lambda i, j: (0, 0))],
        out_specs=pl.BlockSpec((tm, d), row),
        out_shape=jax.ShapeDtypeStruct((t, d), F32),
        scratch_shapes=[pltpu.VMEM((PEER_HEADS, N_KEYS, tm), jnp.uint32),
                        pltpu.VMEM((PEER_HEADS, N_KEYS, tm), jnp.uint32),
                        pltpu.VMEM((PEER_HEADS, N_KEYS, tm), BF16),
                        pltpu.VMEM((PEER_HEADS, N_KEYS, tm), BF16),
                        pltpu.VMEM((d, tm), F32),
                        pltpu.VMEM((eb, tm), BF16)],
        compiler_params=_params(("parallel", "arbitrary")),
        name="peer",
    )(sc, h2, u_bf, vt_bf, x1, mod3, final_g)


def kernel(x, c, w_mod, b_mod, norm1_g, w_in, b_f, w_pool, pool_scale, w_out, norm2_g,
           peer_w_query, peer_sub_keys, peer_u, peer_v, final_g):
    bsz, seq, d = x.shape
    depth = w_mod.shape[0]
    t = bsz * seq
    assert d == ATTN_HEADS * HEAD_DIM and seq % INPROJ_TM == 0 and t % PEER_TM == 0
    x2d = x.reshape(t, d)
    out = x2d
    for l in range(depth):
        mod3 = _mod_call(c, w_mod[l], b_mod[l]).reshape(bsz, N_MOD, d)

        w = w_in[l]
        o_q, o_k, o_v = POOL_W, POOL_W + d, POOL_W + 2 * d
        o_f = POOL_W + 3 * d
        o_g = o_f + ATTN_HEADS
        w_tok = jnp.concatenate([w[:, :o_q], w[:, o_k:o_v], w[:, o_g:]], axis=1).astype(BF16)
        w_feat_t = jnp.concatenate([w[:, o_q:o_k], w[:, o_v:o_f]], axis=1).T.astype(BF16)
        w_f = jnp.pad(w[:, o_f:o_g], ((0, 0), (0, LANES - ATTN_HEADS))).astype(BF16)
        b_f_row = jnp.pad(b_f[l], (0, LANES - ATTN_HEADS)).reshape(1, LANES)
        u, ka, ga, gb, qt, vt = _inproj_call(
            x2d, mod3, norm1_g[l].reshape(1, d), w_tok, w_feat_t, w_f, b_f_row,
            _forget_selectors(), seq)

        yb = _attn_call(qt, ka, vt, bsz, seq)

        n_hp = 2 * PEER_HEADS
        x1, h2, sc = _mix_call(
            u, ga, gb, yb, x2d, mod3, w_pool[l].astype(BF16), pool_scale[l].reshape(1, d),
            w_out[l].astype(BF16), norm2_g[l].reshape(1, d), peer_w_query[l].astype(BF16),
            peer_sub_keys[l].reshape(n_hp, N_KEYS, -1).astype(BF16), seq)

        assert l == depth - 1 == 0
        out = _peer_call(sc, h2, peer_u[l].astype(BF16), peer_v[l].T.astype(BF16),
                         x1, mod3, final_g.reshape(1, d), seq)
        x2d = out
    return out.reshape(bsz, seq, d)
```

```python
import functools

import jax
import jax.numpy as jnp
from jax import lax
from jax.experimental import pallas as pl
from jax.experimental.pallas import tpu as pltpu

F32 = jnp.float32
BF16 = jnp.bfloat16

EPS = 1e-6
NEG_INF = -1e30
LOG2_E = 1.4426950408889634

POOL_WINDOWS = (2, 4, 8, 16)
POOL_GROUP_W = 128
POOL_W = len(POOL_WINDOWS) * POOL_GROUP_W
ATTN_HEADS = 16
HEAD_DIM = 64
PEER_HEADS = 8
N_KEYS = 128
PEER_TOPK = 16
N_MOD = 6

LANES = 128
BF16_ROWS = 16
MXU_COLS = 256
MATMUL_N = 2 * MXU_COLS
VMEM_LIMIT = 56 * 1024 * 1024

INPROJ_TM = 512
ATTN_TQ = 512
ATTN_TK = 512
ATTN_PAIRS = 4
MIX_TM = 512
PEER_TM = 512
PEER_EB = 2048


def _params(sem):
    return pltpu.CompilerParams(dimension_semantics=sem, vmem_limit_bytes=VMEM_LIMIT)


def _split3(a):
    hi = a.astype(BF16)
    r1 = a - hi.astype(F32)
    mid = r1.astype(BF16)
    lo = (r1 - mid.astype(F32)).astype(BF16)
    return hi, mid, lo


def _dot(a, b):
    return jnp.dot(a, b, preferred_element_type=F32)


def _dot_nt(a, b):
    return lax.dot_general(a, b, (((1,), (1,)), ((), ())), preferred_element_type=F32)


def _rms_mod(x, g, shift, scale):
    ms = jnp.mean(x * x, axis=-1, keepdims=True)
    y = x * lax.rsqrt(ms + EPS) * g
    return y * (1.0 + scale) + shift


def _mod_kernel(c_ref, w_ref, b_ref, o_ref):
    c_hi, c_mid, _ = _split3(c_ref[...])
    w_hi, w_mid, _ = _split3(w_ref[...])
    acc = _dot(c_hi, w_hi) + _dot(c_hi, w_mid) + _dot(c_mid, w_hi)
    o_ref[...] = acc + b_ref[...]


def _mod_call(c, w_mod, b_mod):
    bsz, d = c.shape
    n = w_mod.shape[1]
    bn = 1024
    return pl.pallas_call(
        _mod_kernel,
        grid=(n // bn,),
        in_specs=[pl.BlockSpec((bsz, d), lambda j: (0, 0)),
                  pl.BlockSpec((d, bn), lambda j: (0, j)),
                  pl.BlockSpec((1, bn), lambda j: (0, j))],
        out_specs=pl.BlockSpec((bsz, bn), lambda j: (0, j)),
        out_shape=jax.ShapeDtypeStruct((bsz, n), F32),
        compiler_params=_params(("parallel",)),
        name="mod",
    )(c, w_mod, b_mod.reshape(1, n))


def _inproj_kernel(x_ref, mod_ref, g_ref, w_ref, wt_ref, wf_ref, bf_ref, sel_ref,
                   u_ref, k_ref, fa_ref, ga_ref, gb_ref, qt_ref, vt_ref,
                   carry_ref, *, tiles_per_seq, d_model):
    i = pl.program_id(0)
    tm = x_ref.shape[0]
    h = _rms_mod(x_ref[...], g_ref[...], mod_ref[0, 0:1, :], mod_ref[0, 1:2, :])
    hb = h.astype(BF16)

    cw = MATMUL_N
    n_c = d_model // cw

    def proj(c0):
        return _dot(hb, w_ref[:, c0:c0 + cw])

    u_ref[...] = proj(0).astype(BF16)
    base = POOL_W
    for c in range(n_c):
        k_ref[:, c * cw:(c + 1) * cw] = proj(base + c * cw).astype(BF16)
    base += d_model
    for c in range(n_c):
        ga_ref[:, c * cw:(c + 1) * cw] = jax.nn.sigmoid(proj(base + c * cw)).astype(BF16)
    base += d_model
    for c in range(n_c):
        gb_ref[:, c * cw:(c + 1) * cw] = jax.nn.sigmoid(proj(base + c * cw)).astype(BF16)

    scale = HEAD_DIM ** -0.5 * LOG2_E
    for c in range(n_c):
        qt_ref[c * cw:(c + 1) * cw, :] = (_dot_nt(wt_ref[c * cw:(c + 1) * cw, :], hb) * scale).astype(BF16)
    for c in range(n_c):
        r0 = d_model + c * cw
        vt_ref[c * cw:(c + 1) * cw, :] = _dot_nt(wt_ref[r0:r0 + cw, :], hb).astype(BF16)

    zf = _dot(hb, wf_ref[...]) + bf_ref[...]
    logf = jnp.minimum(zf, 0.0) - jnp.log1p(jnp.exp(-jnp.abs(zf)))
    row = lax.broadcasted_iota(jnp.int32, (tm, tm), 0)
    col = lax.broadcasted_iota(jnp.int32, (tm, tm), 1)
    tril = jnp.where(row >= col, 1.0, 0.0).astype(BF16)
    hi, mid, lo = _split3(logf)
    cs = _dot(tril, hi) + _dot(tril, mid) + _dot(tril, lo)

    @pl.when(i % tiles_per_seq == 0)
    def _():
        carry_ref[...] = jnp.zeros_like(carry_ref)

    f_cum = cs + carry_ref[0:1, :]
    carry_ref[...] = jnp.broadcast_to(f_cum[tm - 1:tm, :], carry_ref.shape)
    terms = jnp.concatenate(_split3(f_cum * -LOG2_E), axis=1)
    fa_ref[...] = _dot(terms, sel_ref[...]).astype(BF16)


def _inproj_call(x2d, mod3, norm_g, w_tok, w_feat_t, w_f, b_f, sel, seq):
    t, d = x2d.shape
    tm = INPROJ_TM
    tps = seq // tm
    row = lambda i: (i, 0)
    colb = lambda i: (0, i)
    c2 = lambda i: (0, 0)
    out_shape = (
        jax.ShapeDtypeStruct((t, POOL_W), BF16),
        jax.ShapeDtypeStruct((t, d), BF16), jax.ShapeDtypeStruct((t, LANES), BF16),
        jax.ShapeDtypeStruct((t, d), BF16), jax.ShapeDtypeStruct((t, d), BF16),
        jax.ShapeDtypeStruct((d, t), BF16), jax.ShapeDtypeStruct((d, t), BF16),
    )
    out_specs = (
        pl.BlockSpec((tm, POOL_W), row),
        pl.BlockSpec((tm, d), row), pl.BlockSpec((tm, LANES), row),
        pl.BlockSpec((tm, d), row), pl.BlockSpec((tm, d), row),
        pl.BlockSpec((d, tm), colb), pl.BlockSpec((d, tm), colb),
    )
    return pl.pallas_call(
        functools.partial(_inproj_kernel, tiles_per_seq=tps, d_model=d),
        grid=(t // tm,),
        in_specs=[pl.BlockSpec((tm, d), row),
                  pl.BlockSpec((1, N_MOD, d), lambda i: (i // tps, 0, 0)),
                  pl.BlockSpec((1, d), c2),
                  pl.BlockSpec(w_tok.shape, c2),
                  pl.BlockSpec(w_feat_t.shape, c2),
                  pl.BlockSpec(w_f.shape, c2),
                  pl.BlockSpec(b_f.shape, c2),
                  pl.BlockSpec(sel.shape, c2)],
        out_specs=out_specs,
        out_shape=out_shape,
        scratch_shapes=[pltpu.VMEM((8, LANES), F32)],
        compiler_params=_params(("arbitrary",)),
        name="inproj",
    )(x2d, mod3, norm_g, w_tok, w_feat_t, w_f, b_f, sel)


def _forget_selectors():
    src = jnp.arange(3 * LANES)[:, None]
    dst = jnp.arange(LANES)[None, :]
    r, h = src // LANES, src % LANES
    return jnp.where((h < ATTN_HEADS) & (dst == r * ATTN_HEADS + h), 1.0, 0.0).astype(BF16)


SUM_ROWS = BF16_ROWS


def _attn_kernel(qt_ref, k_ref, fa_ref, vt_ref, o_ref, *, tq, tk):
    seq = k_ref.shape[0]
    rowid = lax.broadcasted_iota(jnp.int32, (LANES, tq), 0)
    krow = lax.broadcasted_iota(jnp.int32, (tk, tq), 0)
    qcol = lax.broadcasted_iota(jnp.int32, (tk, tq), 1)
    n_heads = qt_ref.shape[0] // HEAD_DIM
    first_head = pl.program_id(1) * n_heads

    def aug(hh):
        lane = rowid - (first_head + hh)
        hit = (lane >= 0) & (lane < 3 * ATTN_HEADS) & (lane % ATTN_HEADS == 0)
        return jnp.where(hit, 1.0, 0.0).astype(BF16)

    augs = [aug(hh) for hh in range(n_heads)]
    head = [jnp.where(rowid // HEAD_DIM == h, 1.0, 0.0).astype(BF16) for h in range(2)]
    ones_rows = jnp.ones((SUM_ROWS, tk), BF16)

    def q_tile(qi, carry):
        q0 = pl.multiple_of(qi * tq, tq)
        ws = []
        for hh in range(n_heads):
            pair, h = hh // 2, hh % 2
            qt = qt_ref[pair * LANES:(pair + 1) * LANES, pl.ds(q0, tq)]
            ws.append(jnp.concatenate([qt * head[h], augs[hh]], axis=0))

        def qk(j):
            k0 = pl.multiple_of(j * tk, tk)
            fa = fa_ref[pl.ds(k0, tk), :]
            out = []
            for hh in range(n_heads):
                pair = hh // 2
                ka = jnp.concatenate([k_ref[pl.ds(k0, tk), pair * LANES:(pair + 1) * LANES], fa], axis=1)
                out.append(_dot(ka, ws[hh]))
            return tuple(out)

        def kv_step(j, state, scores, diag):
            k0 = pl.multiple_of(j * tk, tk)
            new = []
            for h in range(n_heads):
                m, acc = state[h]
                s = scores[h]
                if diag is not None:
                    s = jnp.where(krow + diag * tk <= qcol, s, NEG_INF)
                m_new = jnp.maximum(m, jnp.max(s, axis=0, keepdims=True))
                alpha = jnp.exp2(m - m_new)
                p = jnp.exp2(s - m_new)
                vt = vt_ref[h * HEAD_DIM:(h + 1) * HEAD_DIM, pl.ds(k0, tk)]
                acc = alpha * acc + _dot(jnp.concatenate([vt, ones_rows], axis=0), p.astype(BF16))
                new.append((m_new, acc))
            return tuple(new)

        init = tuple((jnp.full((1, tq), NEG_INF, F32), jnp.zeros((HEAD_DIM + SUM_ROWS, tq), F32))
                     for _ in range(n_heads))
        state = lax.fori_loop(0, qi, lambda j, st: kv_step(j, st, qk(j), None), init)
        state = kv_step(qi, state, qk(qi), 0)
        out_t = jnp.concatenate([st[1][:HEAD_DIM] / st[1][HEAD_DIM:HEAD_DIM + 1] for st in state], axis=0)
        o_ref[pl.ds(q0, tq), :] = out_t.T.astype(o_ref.dtype)
        return carry

    lax.fori_loop(0, seq // tq, q_tile, 0)


def _attn_call(qt, k, fa, vt, bsz, seq):
    d, t = qt.shape
    blk = ATTN_PAIRS * LANES
    return pl.pallas_call(
        functools.partial(_attn_kernel, tq=ATTN_TQ, tk=ATTN_TK),
        grid=(bsz, d // blk),
        in_specs=[pl.BlockSpec((blk, seq), lambda b, p: (p, b)),
                  pl.BlockSpec((seq, blk), lambda b, p: (b, p)),
                  pl.BlockSpec((seq, LANES), lambda b, p: (b, 0)),
                  pl.BlockSpec((blk, seq), lambda b, p: (p, b))],
        out_specs=pl.BlockSpec((seq, blk), lambda b, p: (b, p)),
        out_shape=jax.ShapeDtypeStruct((t, d), BF16),
        compiler_params=_params(("parallel", "parallel")),
        name="attn",
    )(qt, k, fa, vt)


def _mix_kernel(uc_ref, up_ref, ga_ref, gb_ref, yb_ref, x_ref, mod_ref, wp_ref, ps_ref,
                wo_ref, g2_ref, wq_ref, sk_ref,
                x1_ref, h2_ref, sc_ref, y_ref, *, tiles_per_seq):
    i = pl.program_id(0)
    tm = x_ref.shape[0]
    halo = LANES
    first = (i % tiles_per_seq) == 0
    pos0 = (i % tiles_per_seq) * tm
    r_d = lax.broadcasted_iota(jnp.int32, (tm, tm), 0)
    c_d = lax.broadcasted_iota(jnp.int32, (tm, tm), 1)
    r_o = lax.broadcasted_iota(jnp.int32, (tm, halo), 0)
    c_o = lax.broadcasted_iota(jnp.int32, (tm, halo), 1)
    pos = pos0 + lax.broadcasted_iota(jnp.int32, (tm, 1), 0)
    gw = wp_ref.shape[2]
    for g, w in enumerate(POOL_WINDOWS):
        lo, hi = g * POOL_GROUP_W, (g + 1) * POOL_GROUP_W
        u_cur = uc_ref[:, lo:hi]
        u_prev = up_ref[tm - halo:tm, lo:hi]
        lag = r_d - c_d
        band_d = jnp.where((lag >= 0) & (lag < w), 1.0, 0.0).astype(BF16)
        band_o = jnp.where((r_o + halo - c_o < w) & jnp.logical_not(first), 1.0, 0.0).astype(BF16)
        wsum = _dot(band_d, u_cur) + _dot(band_o, u_prev)
        cnt = jnp.minimum(pos + 1, w).astype(F32)
        pooled = wsum / cnt - u_cur.astype(F32)
        ya = _dot(pooled.astype(BF16), wp_ref[g]) * ps_ref[:, g * gw:(g + 1) * gw]
        sl = slice(g * gw, (g + 1) * gw)
        y = ga_ref[:, sl].astype(F32) * ya + gb_ref[:, sl].astype(F32) * yb_ref[:, sl].astype(F32)
        y_ref[:, sl] = y.astype(BF16)

    o = _dot(y_ref[...], wo_ref[...])
    x1 = x_ref[...] + mod_ref[0, 2:3, :] * o
    x1_ref[...] = x1
    h2f = _rms_mod(x1, g2_ref[...], mod_ref[0, 3:4, :], mod_ref[0, 4:5, :])
    h2 = h2f.astype(BF16)
    h2_ref[...] = h2f.T.astype(BF16)
    n_hp = sk_ref.shape[0]
    half = sk_ref.shape[2]
    qw = MATMUL_N
    per = qw // half
    for c in range(n_hp // per):
        qp = _dot(h2, wq_ref[:, c * qw:(c + 1) * qw]).astype(BF16)
        for k in range(per):
            hp = c * per + k
            sc_ref[hp] = _dot_nt(sk_ref[hp], qp[:, k * half:(k + 1) * half])


def _mix_call(u, ga, gb, yb, x2d, mod3, w_pool, pool_scale, w_out, norm2_g, w_query, sub_keys, seq):
    t, d = x2d.shape
    tm = MIX_TM
    tps = seq // tm
    row = lambda i: (i, 0)
    c2 = lambda i: (0, 0)
    c3 = lambda i: (0, 0, 0)
    n_hp = sub_keys.shape[0]
    return pl.pallas_call(
        functools.partial(_mix_kernel, tiles_per_seq=tps),
        grid=(t // tm,),
        in_specs=[pl.BlockSpec((tm, POOL_W), row),
                  pl.BlockSpec((tm, POOL_W), lambda i: (jnp.maximum(i - 1, 0), 0)),
                  pl.BlockSpec((tm, d), row), pl.BlockSpec((tm, d), row), pl.BlockSpec((tm, d), row),
                  pl.BlockSpec((tm, d), row),
                  pl.BlockSpec((1, N_MOD, d), lambda i: (i // tps, 0, 0)),
                  pl.BlockSpec(w_pool.shape, c3),
                  pl.BlockSpec((1, d), c2),
                  pl.BlockSpec((d, d), c2),
                  pl.BlockSpec((1, d), c2),
                  pl.BlockSpec(w_query.shape, c2),
                  pl.BlockSpec(sub_keys.shape, c3)],
        out_specs=(pl.BlockSpec((tm, d), row), pl.BlockSpec((d, tm), lambda i: (0, i)),
                   pl.BlockSpec((n_hp, N_KEYS, tm), lambda i: (0, 0, i))),
        out_shape=(jax.ShapeDtypeStruct((t, d), F32), jax.ShapeDtypeStruct((d, t), BF16),
                   jax.ShapeDtypeStruct((n_hp, N_KEYS, t), F32)),
        scratch_shapes=[pltpu.VMEM((tm, d), BF16)],
        compiler_params=_params(("parallel",)),
        name="mix",
    )(u, u, ga, gb, yb, x2d, mod3, w_pool, pool_scale, w_out, norm2_g, w_query, sub_keys)


RANK_CODE_BASE = -(2.0 ** 127)


def _select16(s, first_only):
    n = s.shape[0]
    idx = lax.broadcasted_iota(jnp.int32, s.shape, 0) if first_only else None
    cur = s
    rows = []
    for r in range(PEER_TOPK):
        m = jnp.max(cur, axis=0, keepdims=True)
        rows.append(m)
        hit = cur == m
        if first_only:
            hit = idx == jnp.min(jnp.where(hit, idx, n), axis=0, keepdims=True)
        cur = jnp.where(hit, RANK_CODE_BASE * (1.0 + r / 64.0), cur)
    return rows, cur


def _top16_rows(s):
    rows, cur = _select16(s, first_only=False)
    removed = jnp.sum(jnp.where(cur <= RANK_CODE_BASE, 1.0, 0.0), axis=0, keepdims=True)
    tied = jnp.max(removed) > float(PEER_TOPK)
    rows, cur = lax.cond(tied, lambda: _select16(s, first_only=True), lambda: (rows, cur))
    bits = lax.bitcast_convert_type(cur, jnp.int32)
    coded = ((bits >> 17) & 63).astype(F32)
    rank = jnp.where(cur <= RANK_CODE_BASE, coded, float(PEER_TOPK))
    return rows, rank


def _stack_rows(rows, tm):
    n = len(rows)
    rid = lax.broadcasted_iota(jnp.int32, (n, tm), 0)
    arr = jnp.zeros((n, tm), F32)
    for r, v in enumerate(rows):
        arr = jnp.where(rid == r, v, arr)
    return arr


def _candidates(rows1, arr1_hi, rows2, arr2, combine):
    pieces = [combine(rows1[0], arr2)]
    for a in range(1, 8):
        pieces.append(combine(rows1[a], arr2[0:8]))
    pieces.append(combine(arr1_hi, rows2[0]))
    return jnp.concatenate(pieces, axis=0)


def _dup_bf16(x):
    bits = lax.bitcast_convert_type(x.astype(BF16).astype(F32), jnp.uint32)
    return bits | (bits >> 16)


def _row_tile_bf16(ref, h, row, tm):
    words = jnp.broadcast_to(ref[h, pl.ds(row, 1), :], (8, tm))
    return pltpu.bitcast(words, BF16)


def _peer_stats(sc_ref, e1_ref, lim_ref, e2_ref, r2_ref, h):
    tm = sc_ref.shape[2]
    s1 = sc_ref[2 * h]
    s2 = sc_ref[2 * h + 1]
    t1, rank1 = _top16_rows(s1)
    t2, rank2 = _top16_rows(s2)
    t1_hi = _stack_rows(t1[8:], tm)
    t2_arr = _stack_rows(t2, tm)
    cand = _candidates(t1, t1_hi, t2, t2_arr, lambda a, b: a + b)
    n_c = cand.shape[0]
    cid = lax.broadcasted_iota(jnp.int32, (n_c, tm), 0)
    cur = cand
    z = jnp.zeros((1, tm), F32)
    m0 = None
    for r in range(PEER_TOPK):
        m = jnp.max(cur, axis=0, keepdims=True)
        if r == 0:
            m0 = m
        first = jnp.min(jnp.where(cur == m, cid, n_c), axis=0, keepdims=True)
        cur = jnp.where(cid == first, -jnp.inf, cur)
        z = z + jnp.exp(m - m0)
    sel = jnp.where(cur == -jnp.inf, 1.0, 0.0)
    width = [jnp.sum(sel[0:16], axis=0, keepdims=True)]
    for a in range(1, 8):
        width.append(jnp.sum(sel[8 + 8 * a:16 + 8 * a], axis=0, keepdims=True))
    for a in range(8, PEER_TOPK):
        width.append(sel[64 + a:65 + a])
    lim = jnp.zeros_like(s1)
    for a in range(PEER_TOPK):
        lim = jnp.where(rank1 == float(a), width[a], lim)
    inv_z = 1.0 / z
    e1_ref[h] = _dup_bf16(jnp.exp(s1 - t1[0]) * inv_z)
    lim_ref[h] = _dup_bf16(lim)
    e2_ref[h] = jnp.exp(s2 - t2[0]).astype(BF16)
    r2_ref[h] = rank2.astype(BF16)


def _peer_kernel(sc_ref, h2_ref, u_ref, vt_ref, x1_ref, mod_ref, fg_ref, o_ref,
                 e1_ref, lim_ref, e2_ref, r2_ref, acc_ref, y_ref):
    s = pl.program_id(1)
    n_heads = e1_ref.shape[0]
    eb = u_ref.shape[0]
    tm = h2_ref.shape[1]
    rows_per_blk = eb // N_KEYS
    kc = 2 * N_KEYS

    @pl.when(s == 0)
    def _():
        def head(h, carry):
            _peer_stats(sc_ref, e1_ref, lim_ref, e2_ref, r2_ref, h)
            return carry
        lax.fori_loop(0, n_heads, head, 0)
        acc_ref[...] = jnp.zeros_like(acc_ref)

    def expert_block():
        a_chunks = [_dot(u_ref[c * kc:(c + 1) * kc, :], h2_ref[...]) for c in range(eb // kc)]
        for ii in range(rows_per_blk):
            i_key = s * rows_per_blk + ii
            sub = BF16_ROWS
            g = None
            for h in range(n_heads):
                p = _row_tile_bf16(e1_ref, h, i_key, tm)[None] * e2_ref[h].reshape(N_KEYS // sub, sub, tm)
                lim = _row_tile_bf16(lim_ref, h, i_key, tm)[None]
                term = jnp.where(r2_ref[h].reshape(N_KEYS // sub, sub, tm) < lim, p, jnp.zeros_like(p))
                g = term if g is None else g + term
            g = g.reshape(N_KEYS, tm)
            r_in = ii * N_KEYS % kc
            a = a_chunks[ii * N_KEYS // kc][r_in:r_in + N_KEYS].astype(BF16)
            act = (0.5 * a) * (1.0 + lax.erf(a * (0.5 ** 0.5)))
            y_ref[ii * N_KEYS:(ii + 1) * N_KEYS, :] = act * g
            if (ii + 1) * N_KEYS % kc == 0:
                c0 = (ii + 1) * N_KEYS - kc
                acc_ref[...] += _dot(vt_ref[:, c0:c0 + kc], y_ref[c0:c0 + kc, :])

    expert_block()

    @pl.when(s == pl.num_programs(1) - 1)
    def _():
        peer = acc_ref[...].T
        x2 = x1_ref[...] + mod_ref[0, 5:6, :] * peer
        ms = jnp.mean(x2 * x2, axis=-1, keepdims=True)
        o_ref[...] = x2 * lax.rsqrt(ms + EPS) * fg_ref[...]


def _peer_call(sc, h2, u_bf, vt_bf, x1, mod3, final_g, seq):
    t, d = x1.shape
    tm, eb = PEER_TM, PEER_EB
    tps = seq // tm
    n_hp = sc.shape[0]
    n_exp = u_bf.shape[0]
    row = lambda i, j: (i, 0)
    return pl.pallas_call(
        _peer_kernel,
        grid=(t // tm, n_exp // eb),
        in_specs=[pl.BlockSpec((n_hp, N_KEYS, tm), lambda i, j: (0, 0, i)),
                  pl.BlockSpec((d, tm), lambda i, j: (0, i)),
                  pl.BlockSpec((eb, d), lambda i, j: (j, 0)),
                  pl.BlockSpec((d, eb), lambda i, j: (0, j)),
                  pl.BlockSpec((tm, d), row),
                  pl.BlockSpec((1, N_MOD, d), lambda i, j: (i // tps, 0, 0)),
                  pl.BlockSpec((1, d), lambda i, j: (0, 0))],
        out_specs=pl.BlockSpec((tm, d), row),
        out_shape=jax.ShapeDtypeStruct((t, d), F32),
        scratch_shapes=[pltpu.VMEM((PEER_HEADS, N_KEYS, tm), jnp.uint32),
                        pltpu.VMEM((PEER_HEADS, N_KEYS, tm), jnp.uint32),
                        pltpu.VMEM((PEER_HEADS, N_KEYS, tm), BF16),
                        pltpu.VMEM((PEER_HEADS, N_KEYS, tm), BF16),
                        pltpu.VMEM((d, tm), F32),
                        pltpu.VMEM((eb, tm), BF16)],
        compiler_params=_params(("parallel", "arbitrary")),
        name="peer",
    )(sc, h2, u_bf, vt_bf, x1, mod3, final_g)


def kernel(x, c, w_mod, b_mod, norm1_g, w_in, b_f, w_pool, pool_scale, w_out, norm2_g,
           peer_w_query, peer_sub_keys, peer_u, peer_v, final_g):
    bsz, seq, d = x.shape
    depth = w_mod.shape[0]
    t = bsz * seq
    assert d == ATTN_HEADS * HEAD_DIM and seq % INPROJ_TM == 0 and t % PEER_TM == 0
    x2d = x.reshape(t, d)
    out = x2d
    for l in range(depth):
        mod3 = _mod_call(c, w_mod[l], b_mod[l]).reshape(bsz, N_MOD, d)

        w = w_in[l]
        o_q, o_k, o_v = POOL_W, POOL_W + d, POOL_W + 2 * d
        o_f = POOL_W + 3 * d
        o_g = o_f + ATTN_HEADS
        w_tok = jnp.concatenate([w[:, :o_q], w[:, o_k:o_v], w[:, o_g:]], axis=1).astype(BF16)
        w_feat_t = jnp.concatenate([w[:, o_q:o_k], w[:, o_v:o_f]], axis=1).T.astype(BF16)
        w_f = jnp.pad(w[:, o_f:o_g], ((0, 0), (0, LANES - ATTN_HEADS))).astype(BF16)
        b_f_row = jnp.pad(b_f[l], (0, LANES - ATTN_HEADS)).reshape(1, LANES)
        u, k, fa, ga, gb, qt, vt = _inproj_call(
            x2d, mod3, norm1_g[l].reshape(1, d), w_tok, w_feat_t, w_f, b_f_row,
            _forget_selectors(), seq)

        yb = _attn_call(qt, k, fa, vt, bsz, seq)

        n_hp = 2 * PEER_HEADS
        x1, h2, sc = _mix_call(
            u, ga, gb, yb, x2d, mod3, w_pool[l].astype(BF16), pool_scale[l].reshape(1, d),
            w_out[l].astype(BF16), norm2_g[l].reshape(1, d), peer_w_query[l].astype(BF16),
            peer_sub_keys[l].reshape(n_hp, N_KEYS, -1).astype(BF16), seq)

        assert l == depth - 1 == 0
        out = _peer_call(sc, h2, peer_u[l].astype(BF16), peer_v[l].T.astype(BF16),
                         x1, mod3, final_g.reshape(1, d), seq)
        x2d = out
    return out.reshape(bsz, seq, d)
```

```python
import functools

import jax
import jax.numpy as jnp
from jax import lax
from jax.experimental import pallas as pl
from jax.experimental.pallas import tpu as pltpu

F32 = jnp.float32
BF16 = jnp.bfloat16

EPS = 1e-6
NEG_INF = -1e30
LOG2_E = 1.4426950408889634

POOL_WINDOWS = (2, 4, 8, 16)
POOL_GROUP_W = 128
POOL_W = len(POOL_WINDOWS) * POOL_GROUP_W
ATTN_HEADS = 16
HEAD_DIM = 64
PEER_HEADS = 8
N_KEYS = 128
PEER_TOPK = 16
N_MOD = 6

LANES = 128
BF16_ROWS = 16
MXU_COLS = 256
MATMUL_N = 2 * MXU_COLS
VMEM_LIMIT = 56 * 1024 * 1024

INPROJ_TM = 512
ATTN_TQ = 512
ATTN_TK = 512
ATTN_PAIRS = 4
MIX_TM = 512
PEER_TM = 512
PEER_EB = 2048


def _params(sem):
    return pltpu.CompilerParams(dimension_semantics=sem, vmem_limit_bytes=VMEM_LIMIT)


def _split3(a):
    hi = a.astype(BF16)
    r1 = a - hi.astype(F32)
    mid = r1.astype(BF16)
    lo = (r1 - mid.astype(F32)).astype(BF16)
    return hi, mid, lo


def _dot(a, b):
    return jnp.dot(a, b, preferred_element_type=F32)


def _dot_nt(a, b):
    return lax.dot_general(a, b, (((1,), (1,)), ((), ())), preferred_element_type=F32)


def _rms_mod(x, g, shift, scale):
    ms = jnp.mean(x * x, axis=-1, keepdims=True)
    y = x * lax.rsqrt(ms + EPS) * g
    return y * (1.0 + scale) + shift


def _mod_kernel(c_ref, w_ref, b_ref, o_ref):
    c_hi, c_mid, _ = _split3(c_ref[...])
    w_hi, w_mid, _ = _split3(w_ref[...])
    acc = _dot(c_hi, w_hi) + _dot(c_hi, w_mid) + _dot(c_mid, w_hi)
    o_ref[...] = acc + b_ref[...]


def _mod_call(c, w_mod, b_mod):
    bsz, d = c.shape
    n = w_mod.shape[1]
    bn = 1024
    return pl.pallas_call(
        _mod_kernel,
        grid=(n // bn,),
        in_specs=[pl.BlockSpec((bsz, d), lambda j: (0, 0)),
                  pl.BlockSpec((d, bn), lambda j: (0, j)),
                  pl.BlockSpec((1, bn), lambda j: (0, j))],
        out_specs=pl.BlockSpec((bsz, bn), lambda j: (0, j)),
        out_shape=jax.ShapeDtypeStruct((bsz, n), F32),
        compiler_params=_params(("parallel",)),
        name="mod",
    )(c, w_mod, b_mod.reshape(1, n))


def _inproj_kernel(x_ref, mod_ref, g_ref, w_ref, wt_ref, wf_ref, bf_ref, sel_ref,
                   u_ref, k_ref, fa_ref, ga_ref, gb_ref, qt_ref, vt_ref,
                   carry_ref, *, tiles_per_seq, d_model):
    i = pl.program_id(0)
    tm = x_ref.shape[0]
    h = _rms_mod(x_ref[...], g_ref[...], mod_ref[0, 0:1, :], mod_ref[0, 1:2, :])
    hb = h.astype(BF16)

    cw = MATMUL_N
    n_c = d_model // cw

    def proj(c0):
        return _dot(hb, w_ref[:, c0:c0 + cw])

    u_ref[...] = proj(0).astype(BF16)
    base = POOL_W
    for c in range(n_c):
        k_ref[:, c * cw:(c + 1) * cw] = proj(base + c * cw).astype(BF16)
    base += d_model
    for c in range(n_c):
        ga_ref[:, c * cw:(c + 1) * cw] = jax.nn.sigmoid(proj(base + c * cw)).astype(BF16)
    base += d_model
    for c in range(n_c):
        gb_ref[:, c * cw:(c + 1) * cw] = jax.nn.sigmoid(proj(base + c * cw)).astype(BF16)

    scale = HEAD_DIM ** -0.5 * LOG2_E
    for c in range(n_c):
        qt_ref[c * cw:(c + 1) * cw, :] = (_dot_nt(wt_ref[c * cw:(c + 1) * cw, :], hb) * scale).astype(BF16)
    for c in range(n_c):
        r0 = d_model + c * cw
        vt_ref[c * cw:(c + 1) * cw, :] = _dot_nt(wt_ref[r0:r0 + cw, :], hb).astype(BF16)

    zf = _dot(hb, wf_ref[...]) + bf_ref[...]
    logf = jnp.minimum(zf, 0.0) - jnp.log1p(jnp.exp(-jnp.abs(zf)))
    row = lax.broadcasted_iota(jnp.int32, (tm, tm), 0)
    col = lax.broadcasted_iota(jnp.int32, (tm, tm), 1)
    tril = jnp.where(row >= col, 1.0, 0.0).astype(BF16)
    hi, mid, lo = _split3(logf)
    cs = _dot(tril, hi) + _dot(tril, mid) + _dot(tril, lo)

    @pl.when(i % tiles_per_seq == 0)
    def _():
        carry_ref[...] = jnp.zeros_like(carry_ref)

    f_cum = cs + carry_ref[0:1, :]
    carry_ref[...] = jnp.broadcast_to(f_cum[tm - 1:tm, :], carry_ref.shape)
    terms = jnp.concatenate(_split3(f_cum * -LOG2_E), axis=1)
    fa_ref[...] = _dot(terms, sel_ref[...]).astype(BF16)


def _inproj_call(x2d, mod3, norm_g, w_tok, w_feat_t, w_f, b_f, sel, seq):
    t, d = x2d.shape
    tm = INPROJ_TM
    tps = seq // tm
    row = lambda i: (i, 0)
    colb = lambda i: (0, i)
    c2 = lambda i: (0, 0)
    out_shape = (
        jax.ShapeDtypeStruct((t, POOL_W), BF16),
        jax.ShapeDtypeStruct((t, d), BF16), jax.ShapeDtypeStruct((t, LANES), BF16),
        jax.ShapeDtypeStruct((t, d), BF16), jax.ShapeDtypeStruct((t, d), BF16),
        jax.ShapeDtypeStruct((d, t), BF16), jax.ShapeDtypeStruct((d, t), BF16),
    )
    out_specs = (
        pl.BlockSpec((tm, POOL_W), row),
        pl.BlockSpec((tm, d), row), pl.BlockSpec((tm, LANES), row),
        pl.BlockSpec((tm, d), row), pl.BlockSpec((tm, d), row),
        pl.BlockSpec((d, tm), colb), pl.BlockSpec((d, tm), colb),
    )
    return pl.pallas_call(
        functools.partial(_inproj_kernel, tiles_per_seq=tps, d_model=d),
        grid=(t // tm,),
        in_specs=[pl.BlockSpec((tm, d), row),
                  pl.BlockSpec((1, N_MOD, d), lambda i: (i // tps, 0, 0)),
                  pl.BlockSpec((1, d), c2),
                  pl.BlockSpec(w_tok.shape, c2),
                  pl.BlockSpec(w_feat_t.shape, c2),
                  pl.BlockSpec(w_f.shape, c2),
                  pl.BlockSpec(b_f.shape, c2),
                  pl.BlockSpec(sel.shape, c2)],
        out_specs=out_specs,
        out_shape=out_shape,
        scratch_shapes=[pltpu.VMEM((8, LANES), F32)],
        compiler_params=_params(("arbitrary",)),
        name="inproj",
    )(x2d, mod3, norm_g, w_tok, w_feat_t, w_f, b_f, sel)


def _forget_selectors():
    src = jnp.arange(3 * LANES)[:, None]
    dst = jnp.arange(LANES)[None, :]
    r, h = src // LANES, src % LANES
    return jnp.where((h < ATTN_HEADS) & (dst == r * ATTN_HEADS + h), 1.0, 0.0).astype(BF16)


SUM_ROWS = BF16_ROWS


def _attn_kernel(qt_ref, k_ref, fa_ref, vt_ref, o_ref, *, tq, tk):
    seq = k_ref.shape[0]
    rowid = lax.broadcasted_iota(jnp.int32, (LANES, tq), 0)
    krow = lax.broadcasted_iota(jnp.int32, (tk, tq), 0)
    qcol = lax.broadcasted_iota(jnp.int32, (tk, tq), 1)
    n_heads = qt_ref.shape[0] // HEAD_DIM
    first_head = pl.program_id(1) * n_heads

    def aug(hh):
        lane = rowid - (first_head + hh)
        hit = (lane >= 0) & (lane < 3 * ATTN_HEADS) & (lane % ATTN_HEADS == 0)
        return jnp.where(hit, 1.0, 0.0).astype(BF16)

    augs = [aug(hh) for hh in range(n_heads)]
    head = [jnp.where(rowid // HEAD_DIM == h, 1.0, 0.0).astype(BF16) for h in range(2)]
    ones_rows = jnp.ones((SUM_ROWS, tk), BF16)

    def q_tile(qi, carry):
        q0 = pl.multiple_of(qi * tq, tq)
        ws = []
        for hh in range(n_heads):
            pair, h = hh // 2, hh % 2
            qt = qt_ref[pair * LANES:(pair + 1) * LANES, pl.ds(q0, tq)]
            ws.append(jnp.concatenate([qt * head[h], augs[hh]], axis=0))

        def qk(j):
            k0 = pl.multiple_of(j * tk, tk)
            fa = fa_ref[pl.ds(k0, tk), :]
            out = []
            for hh in range(n_heads):
                pair = hh // 2
                ka = jnp.concatenate([k_ref[pl.ds(k0, tk), pair * LANES:(pair + 1) * LANES], fa], axis=1)
                out.append(_dot(ka, ws[hh]))
            return tuple(out)

        def kv_step(j, state, scores, diag):
            k0 = pl.multiple_of(j * tk, tk)
            new = []
            for h in range(n_heads):
                m, acc = state[h]
                s = scores[h]
                if diag is not None:
                    s = jnp.where(krow + diag * tk <= qcol, s, NEG_INF)
                m_new = jnp.maximum(m, jnp.max(s, axis=0, keepdims=True))
                alpha = jnp.exp2(m - m_new)
                p = jnp.exp2(s - m_new)
                vt = vt_ref[h * HEAD_DIM:(h + 1) * HEAD_DIM, pl.ds(k0, tk)]
                acc = alpha * acc + _dot(jnp.concatenate([vt, ones_rows], axis=0), p.astype(BF16))
                new.append((m_new, acc))
            return tuple(new)

        init = tuple((jnp.full((1, tq), NEG_INF, F32), jnp.zeros((HEAD_DIM + SUM_ROWS, tq), F32))
                     for _ in range(n_heads))
        state = lax.fori_loop(0, qi, lambda j, st: kv_step(j, st, qk(j), None), init)
        state = kv_step(qi, state, qk(qi), 0)
        out_t = jnp.concatenate([st[1][:HEAD_DIM] / st[1][HEAD_DIM:HEAD_DIM + 1] for st in state], axis=0)
        o_ref[pl.ds(q0, tq), :] = out_t.T.astype(o_ref.dtype)
        return carry

    lax.fori_loop(0, seq // tq, q_tile, 0)


def _attn_call(qt, k, fa, vt, bsz, seq):
    d, t = qt.shape
    blk = ATTN_PAIRS * LANES
    return pl.pallas_call(
        functools.partial(_attn_kernel, tq=ATTN_TQ, tk=ATTN_TK),
        grid=(bsz, d // blk),
        in_specs=[pl.BlockSpec((blk, seq), lambda b, p: (p, b)),
                  pl.BlockSpec((seq, blk), lambda b, p: (b, p)),
                  pl.BlockSpec((seq, LANES), lambda b, p: (b, 0)),
                  pl.BlockSpec((blk, seq), lambda b, p: (p, b))],
        out_specs=pl.BlockSpec((seq, blk), lambda b, p: (b, p)),
        out_shape=jax.ShapeDtypeStruct((t, d), BF16),
        compiler_params=_params(("parallel", "parallel")),
        name="attn",
    )(qt, k, fa, vt)


def _mix_kernel(uc_ref, up_ref, ga_ref, gb_ref, yb_ref, x_ref, mod_ref, wp_ref, ps_ref,
                wo_ref, g2_ref,
                x1_ref, h2_ref, y_ref, *, tiles_per_seq):
    i = pl.program_id(0)
    tm = x_ref.shape[0]
    halo = LANES
    first = (i % tiles_per_seq) == 0
    pos0 = (i % tiles_per_seq) * tm
    r_d = lax.broadcasted_iota(jnp.int32, (tm, tm), 0)
    c_d = lax.broadcasted_iota(jnp.int32, (tm, tm), 1)
    r_o = lax.broadcasted_iota(jnp.int32, (tm, halo), 0)
    c_o = lax.broadcasted_iota(jnp.int32, (tm, halo), 1)
    pos = pos0 + lax.broadcasted_iota(jnp.int32, (tm, 1), 0)
    gw = wp_ref.shape[2]
    for g, w in enumerate(POOL_WINDOWS):
        lo, hi = g * POOL_GROUP_W, (g + 1) * POOL_GROUP_W
        u_cur = uc_ref[:, lo:hi]
        u_prev = up_ref[tm - halo:tm, lo:hi]
        lag = r_d - c_d
        band_d = jnp.where((lag >= 0) & (lag < w), 1.0, 0.0).astype(BF16)
        band_o = jnp.where((r_o + halo - c_o < w) & jnp.logical_not(first), 1.0, 0.0).astype(BF16)
        wsum = _dot(band_d, u_cur) + _dot(band_o, u_prev)
        cnt = jnp.minimum(pos + 1, w).astype(F32)
        pooled = wsum / cnt - u_cur.astype(F32)
        ya = _dot(pooled.astype(BF16), wp_ref[g]) * ps_ref[:, g * gw:(g + 1) * gw]
        sl = slice(g * gw, (g + 1) * gw)
        y = ga_ref[:, sl].astype(F32) * ya + gb_ref[:, sl].astype(F32) * yb_ref[:, sl].astype(F32)
        y_ref[:, sl] = y.astype(BF16)

    o = _dot(y_ref[...], wo_ref[...])
    x1 = x_ref[...] + mod_ref[0, 2:3, :] * o
    x1_ref[...] = x1
    h2f = _rms_mod(x1, g2_ref[...], mod_ref[0, 3:4, :], mod_ref[0, 4:5, :])
    h2_ref[...] = h2f.T.astype(BF16)


def _mix_call(u, ga, gb, yb, x2d, mod3, w_pool, pool_scale, w_out, norm2_g, seq):
    t, d = x2d.shape
    tm = MIX_TM
    tps = seq // tm
    row = lambda i: (i, 0)
    c2 = lambda i: (0, 0)
    c3 = lambda i: (0, 0, 0)
    return pl.pallas_call(
        functools.partial(_mix_kernel, tiles_per_seq=tps),
        grid=(t // tm,),
        in_specs=[pl.BlockSpec((tm, POOL_W), row),
                  pl.BlockSpec((tm, POOL_W), lambda i: (jnp.maximum(i - 1, 0), 0)),
                  pl.BlockSpec((tm, d), row), pl.BlockSpec((tm, d), row), pl.BlockSpec((tm, d), row),
                  pl.BlockSpec((tm, d), row),
                  pl.BlockSpec((1, N_MOD, d), lambda i: (i // tps, 0, 0)),
                  pl.BlockSpec(w_pool.shape, c3),
                  pl.BlockSpec((1, d), c2),
                  pl.BlockSpec((d, d), c2),
                  pl.BlockSpec((1, d), c2)],
        out_specs=(pl.BlockSpec((tm, d), row), pl.BlockSpec((d, tm), lambda i: (0, i))),
        out_shape=(jax.ShapeDtypeStruct((t, d), F32), jax.ShapeDtypeStruct((d, t), BF16)),
        scratch_shapes=[pltpu.VMEM((tm, d), BF16)],
        compiler_params=_params(("parallel",)),
        name="mix",
    )(u, u, ga, gb, yb, x2d, mod3, w_pool, pool_scale, w_out, norm2_g)


RANK_CODE_BASE = -(2.0 ** 127)


def _select16(s, first_only):
    n = s.shape[0]
    idx = lax.broadcasted_iota(jnp.int32, s.shape, 0) if first_only else None
    cur = s
    rows = []
    for r in range(PEER_TOPK):
        m = jnp.max(cur, axis=0, keepdims=True)
        rows.append(m)
        hit = cur == m
        if first_only:
            hit = idx == jnp.min(jnp.where(hit, idx, n), axis=0, keepdims=True)
        cur = jnp.where(hit, RANK_CODE_BASE * (1.0 + r / 64.0), cur)
    return rows, cur


def _top16_rows(s):
    rows, cur = _select16(s, first_only=False)
    removed = jnp.sum(jnp.where(cur <= RANK_CODE_BASE, 1.0, 0.0), axis=0, keepdims=True)
    tied = jnp.max(removed) > float(PEER_TOPK)
    rows, cur = lax.cond(tied, lambda: _select16(s, first_only=True), lambda: (rows, cur))
    bits = lax.bitcast_convert_type(cur, jnp.int32)
    coded = ((bits >> 17) & 63).astype(F32)
    rank = jnp.where(cur <= RANK_CODE_BASE, coded, float(PEER_TOPK))
    return rows, rank


def _stack_rows(rows, tm):
    n = len(rows)
    rid = lax.broadcasted_iota(jnp.int32, (n, tm), 0)
    arr = jnp.zeros((n, tm), F32)
    for r, v in enumerate(rows):
        arr = jnp.where(rid == r, v, arr)
    return arr


def _candidates(rows1, arr1_hi, rows2, arr2, combine):
    pieces = [combine(rows1[0], arr2)]
    for a in range(1, 8):
        pieces.append(combine(rows1[a], arr2[0:8]))
    pieces.append(combine(arr1_hi, rows2[0]))
    return jnp.concatenate(pieces, axis=0)


def _dup_bf16(x):
    bits = lax.bitcast_convert_type(x.astype(BF16).astype(F32), jnp.uint32)
    return bits | (bits >> 16)


def _row_tile_bf16(ref, h, row, tm):
    words = jnp.broadcast_to(ref[h, pl.ds(row, 1), :], (8, tm))
    return pltpu.bitcast(words, BF16)


def _peer_stats(s1, s2, e1_ref, lim_ref, e2_ref, r2_ref, h):
    tm = s1.shape[1]
    t1, rank1 = _top16_rows(s1)
    t2, rank2 = _top16_rows(s2)
    t1_hi = _stack_rows(t1[8:], tm)
    t2_arr = _stack_rows(t2, tm)
    cand = _candidates(t1, t1_hi, t2, t2_arr, lambda a, b: a + b)
    n_c = cand.shape[0]
    cid = lax.broadcasted_iota(jnp.int32, (n_c, tm), 0)
    cur = cand
    z = jnp.zeros((1, tm), F32)
    m0 = None
    for r in range(PEER_TOPK):
        m = jnp.max(cur, axis=0, keepdims=True)
        if r == 0:
            m0 = m
        first = jnp.min(jnp.where(cur == m, cid, n_c), axis=0, keepdims=True)
        cur = jnp.where(cid == first, -jnp.inf, cur)
        z = z + jnp.exp(m - m0)
    sel = jnp.where(cur == -jnp.inf, 1.0, 0.0)
    width = [jnp.sum(sel[0:16], axis=0, keepdims=True)]
    for a in range(1, 8):
        width.append(jnp.sum(sel[8 + 8 * a:16 + 8 * a], axis=0, keepdims=True))
    for a in range(8, PEER_TOPK):
        width.append(sel[64 + a:65 + a])
    lim = jnp.zeros_like(s1)
    for a in range(PEER_TOPK):
        lim = jnp.where(rank1 == float(a), width[a], lim)
    inv_z = 1.0 / z
    e1_ref[h] = _dup_bf16(jnp.exp(s1 - t1[0]) * inv_z)
    lim_ref[h] = _dup_bf16(lim)
    e2_ref[h] = jnp.exp(s2 - t2[0]).astype(BF16)
    r2_ref[h] = rank2.astype(BF16)


def _peer_kernel(wq_ref, sk_ref, h2_ref, u_ref, vt_ref, x1_ref, mod_ref, fg_ref, o_ref,
                 e1_ref, lim_ref, e2_ref, r2_ref, acc_ref, y_ref):
    s = pl.program_id(1)
    n_heads = e1_ref.shape[0]
    eb = u_ref.shape[0]
    tm = h2_ref.shape[1]
    rows_per_blk = eb // N_KEYS
    kc = 2 * N_KEYS

    @pl.when(s == 0)
    def _():
        half = sk_ref.shape[2]

        def scores(h):
            r0 = pl.multiple_of(h * 2 * half, 2 * half)
            qp = _dot(wq_ref[pl.ds(r0, 2 * half), :], h2_ref[...]).astype(BF16)
            return _dot(sk_ref[2 * h], qp[:half]), _dot(sk_ref[2 * h + 1], qp[half:])

        def head(h, carry):
            s1, s2 = carry
            nxt = scores(jnp.minimum(h + 1, n_heads - 1))
            _peer_stats(s1, s2, e1_ref, lim_ref, e2_ref, r2_ref, h)
            return nxt
        lax.fori_loop(0, n_heads, head, scores(0))
        acc_ref[...] = jnp.zeros_like(acc_ref)

    def expert_block():
        a_chunks = [_dot(u_ref[c * kc:(c + 1) * kc, :], h2_ref[...]) for c in range(eb // kc)]
        for ii in range(rows_per_blk):
            i_key = s * rows_per_blk + ii
            sub = BF16_ROWS
            g = None
            for h in range(n_heads):
                p = _row_tile_bf16(e1_ref, h, i_key, tm)[None] * e2_ref[h].reshape(N_KEYS // sub, sub, tm)
                lim = _row_tile_bf16(lim_ref, h, i_key, tm)[None]
                term = jnp.where(r2_ref[h].reshape(N_KEYS // sub, sub, tm) < lim, p, jnp.zeros_like(p))
                g = term if g is None else g + term
            g = g.reshape(N_KEYS, tm)
            r_in = ii * N_KEYS % kc
            a = a_chunks[ii * N_KEYS // kc][r_in:r_in + N_KEYS].astype(BF16)
            act = (0.5 * a) * (1.0 + lax.erf(a * (0.5 ** 0.5)))
            y_ref[ii * N_KEYS:(ii + 1) * N_KEYS, :] = act * g
            if (ii + 1) * N_KEYS % kc == 0:
                c0 = (ii + 1) * N_KEYS - kc
                acc_ref[...] += _dot(vt_ref[:, c0:c0 + kc], y_ref[c0:c0 + kc, :])

    expert_block()

    @pl.when(s == pl.num_programs(1) - 1)
    def _():
        peer = acc_ref[...].T
        x2 = x1_ref[...] + mod_ref[0, 5:6, :] * peer
        ms = jnp.mean(x2 * x2, axis=-1, keepdims=True)
        o_ref[...] = x2 * lax.rsqrt(ms + EPS) * fg_ref[...]


def _peer_call(wq_t, sub_keys, h2, u_bf, vt_bf, x1, mod3, final_g, seq):
    t, d = x1.shape
    tm, eb = PEER_TM, PEER_EB
    tps = seq // tm
    n_exp = u_bf.shape[0]
    row = lambda i, j: (i, 0)
    return pl.pallas_call(
        _peer_kernel,
        grid=(t // tm, n_exp // eb),
        in_specs=[pl.BlockSpec(wq_t.shape, lambda i, j: (0, 0)),
                  pl.BlockSpec(sub_keys.shape, lambda i, j: (0, 0, 0)),
                  pl.BlockSpec((d, tm), lambda i, j: (0, i)),
                  pl.BlockSpec((eb, d), lambda i, j: (j, 0)),
                  pl.BlockSpec((d, eb), lambda i, j: (0, j)),
                  pl.BlockSpec((tm, d), row),
                  pl.BlockSpec((1, N_MOD, d), lambda i, j: (i // tps, 0, 0)),
                  pl.BlockSpec((1, d), lambda i, j: (0, 0))],
        out_specs=pl.BlockSpec((tm, d), row),
        out_shape=jax.ShapeDtypeStruct((t, d), F32),
        scratch_shapes=[pltpu.VMEM((PEER_HEADS, N_KEYS, tm), jnp.uint32),
                        pltpu.VMEM((PEER_HEADS, N_KEYS, tm), jnp.uint32),
                        pltpu.VMEM((PEER_HEADS, N_KEYS, tm), BF16),
                        pltpu.VMEM((PEER_HEADS, N_KEYS, tm), BF16),
                        pltpu.VMEM((d, tm), F32),
                        pltpu.VMEM((eb, tm), BF16)],
        compiler_params=_params(("parallel", "arbitrary")),
        name="peer",
    )(wq_t, sub_keys, h2, u_bf, vt_bf, x1, mod3, final_g)


def kernel(x, c, w_mod, b_mod, norm1_g, w_in, b_f, w_pool, pool_scale, w_out, norm2_g,
           peer_w_query, peer_sub_keys, peer_u, peer_v, final_g):
    bsz, seq, d = x.shape
    depth = w_mod.shape[0]
    t = bsz * seq
    assert d == ATTN_HEADS * HEAD_DIM and seq % INPROJ_TM == 0 and t % PEER_TM == 0
    x2d = x.reshape(t, d)
    out = x2d
    for l in range(depth):
        mod3 = _mod_call(c, w_mod[l], b_mod[l]).reshape(bsz, N_MOD, d)

        w = w_in[l]
        o_q, o_k, o_v = POOL_W, POOL_W + d, POOL_W + 2 * d
        o_f = POOL_W + 3 * d
        o_g = o_f + ATTN_HEADS
        w_tok = jnp.concatenate([w[:, :o_q], w[:, o_k:o_v], w[:, o_g:]], axis=1).astype(BF16)
        w_feat_t = jnp.concatenate([w[:, o_q:o_k], w[:, o_v:o_f]], axis=1).T.astype(BF16)
        w_f = jnp.pad(w[:, o_f:o_g], ((0, 0), (0, LANES - ATTN_HEADS))).astype(BF16)
        b_f_row = jnp.pad(b_f[l], (0, LANES - ATTN_HEADS)).reshape(1, LANES)
        u, k, fa, ga, gb, qt, vt = _inproj_call(
            x2d, mod3, norm1_g[l].reshape(1, d), w_tok, w_feat_t, w_f, b_f_row,
            _forget_selectors(), seq)

        yb = _attn_call(qt, k, fa, vt, bsz, seq)

        n_hp = 2 * PEER_HEADS
        x1, h2 = _mix_call(
            u, ga, gb, yb, x2d, mod3, w_pool[l].astype(BF16), pool_scale[l].reshape(1, d),
            w_out[l].astype(BF16), norm2_g[l].reshape(1, d), seq)

        assert l == depth - 1 == 0
        out = _peer_call(peer_w_query[l].T.astype(BF16),
                         peer_sub_keys[l].reshape(n_hp, N_KEYS, -1).astype(BF16), h2, peer_u[l].astype(BF16), peer_v[l].T.astype(BF16),
                         x1, mod3, final_g.reshape(1, d), seq)
        x2d = out
    return out.reshape(bsz, seq, d)
```

```python
import functools

import jax
import jax.numpy as jnp
from jax import lax
from jax.experimental import pallas as pl
from jax.experimental.pallas import tpu as pltpu

F32 = jnp.float32
BF16 = jnp.bfloat16

EPS = 1e-6
NEG_INF = -1e30
LOG2_E = 1.4426950408889634

POOL_WINDOWS = (2, 4, 8, 16)
POOL_GROUP_W = 128
POOL_W = len(POOL_WINDOWS) * POOL_GROUP_W
ATTN_HEADS = 16
HEAD_DIM = 64
PEER_HEADS = 8
N_KEYS = 128
PEER_TOPK = 16
N_MOD = 6

LANES = 128
BF16_ROWS = 16
MXU_COLS = 256
MATMUL_N = 2 * MXU_COLS
VMEM_LIMIT = 56 * 1024 * 1024
ONE_BUFFER = pl.Buffered(1)

INPROJ_TM = 512
ATTN_TQ = 512
ATTN_TK = 512
ATTN_PAIRS = 4
MIX_TM = 512
PEER_TM = 512
PEER_EB = 2048


def _params(sem):
    return pltpu.CompilerParams(dimension_semantics=sem, vmem_limit_bytes=VMEM_LIMIT)


def _split3(a):
    hi = a.astype(BF16)
    r1 = a - hi.astype(F32)
    mid = r1.astype(BF16)
    lo = (r1 - mid.astype(F32)).astype(BF16)
    return hi, mid, lo


def _dot(a, b):
    return jnp.dot(a, b, preferred_element_type=F32)


def _dot_nt(a, b):
    return lax.dot_general(a, b, (((1,), (1,)), ((), ())), preferred_element_type=F32)


def _rms_mod(x, g, shift, scale):
    ms = jnp.mean(x * x, axis=-1, keepdims=True)
    y = x * lax.rsqrt(ms + EPS) * g
    return y * (1.0 + scale) + shift


def _mod_kernel(c_ref, w_ref, b_ref, o_ref):
    c_hi, c_mid, _ = _split3(c_ref[...])
    w_hi, w_mid, _ = _split3(w_ref[...])
    acc = _dot(c_hi, w_hi) + _dot(c_hi, w_mid) + _dot(c_mid, w_hi)
    o_ref[...] = acc + b_ref[...]


def _mod_call(c, w_mod, b_mod):
    bsz, d = c.shape
    n = w_mod.shape[1]
    bn = 1024
    return pl.pallas_call(
        _mod_kernel,
        grid=(n // bn,),
        in_specs=[pl.BlockSpec((bsz, d), lambda j: (0, 0)),
                  pl.BlockSpec((d, bn), lambda j: (0, j)),
                  pl.BlockSpec((1, bn), lambda j: (0, j))],
        out_specs=pl.BlockSpec((bsz, bn), lambda j: (0, j)),
        out_shape=jax.ShapeDtypeStruct((bsz, n), F32),
        compiler_params=_params(("parallel",)),
        name="mod",
    )(c, w_mod, b_mod.reshape(1, n))


def _inproj_kernel(x_ref, mod_ref, g_ref, w_ref, wt_ref, wf_ref, bf_ref, sel_ref,
                   u_ref, k_ref, fa_ref, ga_ref, gb_ref, qt_ref, vt_ref,
                   carry_ref, *, tiles_per_seq, d_model):
    i = pl.program_id(0)
    tm = x_ref.shape[0]
    h = _rms_mod(x_ref[...], g_ref[...], mod_ref[0, 0:1, :], mod_ref[0, 1:2, :])
    hb = h.astype(BF16)

    cw = MATMUL_N
    n_c = d_model // cw

    def proj(c0):
        return _dot(hb, w_ref[:, c0:c0 + cw])

    u_ref[...] = proj(0).astype(BF16)
    base = POOL_W
    for c in range(n_c):
        k_ref[:, c * cw:(c + 1) * cw] = proj(base + c * cw).astype(BF16)
    base += d_model
    for c in range(n_c):
        ga_ref[:, c * cw:(c + 1) * cw] = jax.nn.sigmoid(proj(base + c * cw)).astype(BF16)
    base += d_model
    for c in range(n_c):
        gb_ref[:, c * cw:(c + 1) * cw] = jax.nn.sigmoid(proj(base + c * cw)).astype(BF16)

    scale = HEAD_DIM ** -0.5 * LOG2_E
    for c in range(n_c):
        qt_ref[c * cw:(c + 1) * cw, :] = (_dot_nt(wt_ref[c * cw:(c + 1) * cw, :], hb) * scale).astype(BF16)
    for c in range(n_c):
        r0 = d_model + c * cw
        vt_ref[c * cw:(c + 1) * cw, :] = _dot_nt(wt_ref[r0:r0 + cw, :], hb).astype(BF16)

    zf = _dot(hb, wf_ref[...]) + bf_ref[...]
    logf = jnp.minimum(zf, 0.0) - jnp.log1p(jnp.exp(-jnp.abs(zf)))
    row = lax.broadcasted_iota(jnp.int32, (tm, tm), 0)
    col = lax.broadcasted_iota(jnp.int32, (tm, tm), 1)
    tril = jnp.where(row >= col, 1.0, 0.0).astype(BF16)
    hi, mid, lo = _split3(logf)
    cs = _dot(tril, hi) + _dot(tril, mid) + _dot(tril, lo)

    @pl.when(i % tiles_per_seq == 0)
    def _():
        carry_ref[...] = jnp.zeros_like(carry_ref)

    f_cum = cs + carry_ref[0:1, :]
    carry_ref[...] = jnp.broadcast_to(f_cum[tm - 1:tm, :], carry_ref.shape)
    terms = jnp.concatenate(_split3(f_cum * -LOG2_E), axis=1)
    fa_ref[...] = _dot(terms, sel_ref[...]).astype(BF16)


def _inproj_call(x2d, mod3, norm_g, w_tok, w_feat_t, w_f, b_f, sel, seq):
    t, d = x2d.shape
    tm = INPROJ_TM
    tps = seq // tm
    row = lambda i: (i, 0)
    colb = lambda i: (0, i)
    c2 = lambda i: (0, 0)
    out_shape = (
        jax.ShapeDtypeStruct((t, POOL_W), BF16),
        jax.ShapeDtypeStruct((t, d), BF16), jax.ShapeDtypeStruct((t, LANES), BF16),
        jax.ShapeDtypeStruct((t, d), BF16), jax.ShapeDtypeStruct((t, d), BF16),
        jax.ShapeDtypeStruct((d, t), BF16), jax.ShapeDtypeStruct((d, t), BF16),
    )
    out_specs = (
        pl.BlockSpec((tm, POOL_W), row),
        pl.BlockSpec((tm, d), row), pl.BlockSpec((tm, LANES), row),
        pl.BlockSpec((tm, d), row), pl.BlockSpec((tm, d), row),
        pl.BlockSpec((d, tm), colb), pl.BlockSpec((d, tm), colb),
    )
    return pl.pallas_call(
        functools.partial(_inproj_kernel, tiles_per_seq=tps, d_model=d),
        grid=(t // tm,),
        in_specs=[pl.BlockSpec((tm, d), row),
                  pl.BlockSpec((1, N_MOD, d), lambda i: (i // tps, 0, 0)),
                  pl.BlockSpec((1, d), c2),
                  pl.BlockSpec(w_tok.shape, c2, pipeline_mode=ONE_BUFFER),
                  pl.BlockSpec(w_feat_t.shape, c2, pipeline_mode=ONE_BUFFER),
                  pl.BlockSpec(w_f.shape, c2, pipeline_mode=ONE_BUFFER),
                  pl.BlockSpec(b_f.shape, c2),
                  pl.BlockSpec(sel.shape, c2, pipeline_mode=ONE_BUFFER)],
        out_specs=out_specs,
        out_shape=out_shape,
        scratch_shapes=[pltpu.VMEM((8, LANES), F32)],
        compiler_params=_params(("arbitrary",)),
        name="inproj",
    )(x2d, mod3, norm_g, w_tok, w_feat_t, w_f, b_f, sel)


def _forget_selectors():
    src = jnp.arange(3 * LANES)[:, None]
    dst = jnp.arange(LANES)[None, :]
    r, h = src // LANES, src % LANES
    return jnp.where((h < ATTN_HEADS) & (dst == r * ATTN_HEADS + h), 1.0, 0.0).astype(BF16)


SUM_ROWS = BF16_ROWS


def _attn_kernel(qt_ref, k_ref, fa_ref, vt_ref, o_ref, *, tq, tk):
    seq = k_ref.shape[0]
    rowid = lax.broadcasted_iota(jnp.int32, (LANES, tq), 0)
    krow = lax.broadcasted_iota(jnp.int32, (tk, tq), 0)
    qcol = lax.broadcasted_iota(jnp.int32, (tk, tq), 1)
    n_heads = qt_ref.shape[0] // HEAD_DIM
    first_head = pl.program_id(1) * n_heads

    def aug(hh):
        lane = rowid - (first_head + hh)
        hit = (lane >= 0) & (lane < 3 * ATTN_HEADS) & (lane % ATTN_HEADS == 0)
        return jnp.where(hit, 1.0, 0.0).astype(BF16)

    augs = [aug(hh) for hh in range(n_heads)]
    head = [jnp.where(rowid // HEAD_DIM == h, 1.0, 0.0).astype(BF16) for h in range(2)]
    ones_rows = jnp.ones((SUM_ROWS, tk), BF16)

    def q_tile(qi, carry):
        q0 = pl.multiple_of(qi * tq, tq)
        ws = []
        for hh in range(n_heads):
            pair, h = hh // 2, hh % 2
            qt = qt_ref[pair * LANES:(pair + 1) * LANES, pl.ds(q0, tq)]
            ws.append(jnp.concatenate([qt * head[h], augs[hh]], axis=0))

        def qk(j):
            k0 = pl.multiple_of(j * tk, tk)
            fa = fa_ref[pl.ds(k0, tk), :]
            out = []
            for hh in range(n_heads):
                pair = hh // 2
                ka = jnp.concatenate([k_ref[pl.ds(k0, tk), pair * LANES:(pair + 1) * LANES], fa], axis=1)
                out.append(_dot(ka, ws[hh]))
            return tuple(out)

        def kv_step(j, state, scores, diag):
            k0 = pl.multiple_of(j * tk, tk)
            new = []
            for h in range(n_heads):
                m, acc = state[h]
                s = scores[h]
                if diag is not None:
                    s = jnp.where(krow + diag * tk <= qcol, s, NEG_INF)
                m_new = jnp.maximum(m, jnp.max(s, axis=0, keepdims=True))
                alpha = jnp.exp2(m - m_new)
                p = jnp.exp2(s - m_new)
                vt = vt_ref[h * HEAD_DIM:(h + 1) * HEAD_DIM, pl.ds(k0, tk)]
                acc = alpha * acc + _dot(jnp.concatenate([vt, ones_rows], axis=0), p.astype(BF16))
                new.append((m_new, acc))
            return tuple(new)

        init = tuple((jnp.full((1, tq), NEG_INF, F32), jnp.zeros((HEAD_DIM + SUM_ROWS, tq), F32))
                     for _ in range(n_heads))
        state = lax.fori_loop(0, qi, lambda j, st: kv_step(j, st, qk(j), None), init)
        state = kv_step(qi, state, qk(qi), 0)
        out_t = jnp.concatenate([st[1][:HEAD_DIM] / st[1][HEAD_DIM:HEAD_DIM + 1] for st in state], axis=0)
        o_ref[pl.ds(q0, tq), :] = out_t.T.astype(o_ref.dtype)
        return carry

    lax.fori_loop(0, seq // tq, q_tile, 0)


def _attn_call(qt, k, fa, vt, bsz, seq):
    d, t = qt.shape
    blk = ATTN_PAIRS * LANES
    return pl.pallas_call(
        functools.partial(_attn_kernel, tq=ATTN_TQ, tk=ATTN_TK),
        grid=(bsz, d // blk),
        in_specs=[pl.BlockSpec((blk, seq), lambda b, p: (p, b)),
                  pl.BlockSpec((seq, blk), lambda b, p: (b, p)),
                  pl.BlockSpec((seq, LANES), lambda b, p: (b, 0)),
                  pl.BlockSpec((blk, seq), lambda b, p: (p, b))],
        out_specs=pl.BlockSpec((seq, blk), lambda b, p: (b, p)),
        out_shape=jax.ShapeDtypeStruct((t, d), BF16),
        compiler_params=_params(("parallel", "parallel")),
        name="attn",
    )(qt, k, fa, vt)


def _mix_kernel(uc_ref, up_ref, ga_ref, gb_ref, yb_ref, x_ref, mod_ref, wp_ref, ps_ref,
                wo_ref, g2_ref,
                x1_ref, h2_ref, y_ref, *, tiles_per_seq):
    i = pl.program_id(0)
    tm = x_ref.shape[0]
    halo = LANES
    first = (i % tiles_per_seq) == 0
    pos0 = (i % tiles_per_seq) * tm
    r_d = lax.broadcasted_iota(jnp.int32, (tm, tm), 0)
    c_d = lax.broadcasted_iota(jnp.int32, (tm, tm), 1)
    r_o = lax.broadcasted_iota(jnp.int32, (tm, halo), 0)
    c_o = lax.broadcasted_iota(jnp.int32, (tm, halo), 1)
    pos = pos0 + lax.broadcasted_iota(jnp.int32, (tm, 1), 0)
    gw = wp_ref.shape[2]
    for g, w in enumerate(POOL_WINDOWS):
        lo, hi = g * POOL_GROUP_W, (g + 1) * POOL_GROUP_W
        u_cur = uc_ref[:, lo:hi]
        u_prev = up_ref[tm - halo:tm, lo:hi]
        lag = r_d - c_d
        band_d = jnp.where((lag >= 0) & (lag < w), 1.0, 0.0).astype(BF16)
        band_o = jnp.where((r_o + halo - c_o < w) & jnp.logical_not(first), 1.0, 0.0).astype(BF16)
        wsum = _dot(band_d, u_cur) + _dot(band_o, u_prev)
        cnt = jnp.minimum(pos + 1, w).astype(F32)
        pooled = wsum / cnt - u_cur.astype(F32)
        ya = _dot(pooled.astype(BF16), wp_ref[g]) * ps_ref[:, g * gw:(g + 1) * gw]
        sl = slice(g * gw, (g + 1) * gw)
        y = ga_ref[:, sl].astype(F32) * ya + gb_ref[:, sl].astype(F32) * yb_ref[:, sl].astype(F32)
        y_ref[:, sl] = y.astype(BF16)

    o = _dot(y_ref[...], wo_ref[...])
    x1 = x_ref[...] + mod_ref[0, 2:3, :] * o
    x1_ref[...] = x1
    h2f = _rms_mod(x1, g2_ref[...], mod_ref[0, 3:4, :], mod_ref[0, 4:5, :])
    h2_ref[...] = h2f.T.astype(BF16)


def _mix_call(u, ga, gb, yb, x2d, mod3, w_pool, pool_scale, w_out, norm2_g, seq):
    t, d = x2d.shape
    tm = MIX_TM
    tps = seq // tm
    row = lambda i: (i, 0)
    c2 = lambda i: (0, 0)
    c3 = lambda i: (0, 0, 0)
    return pl.pallas_call(
        functools.partial(_mix_kernel, tiles_per_seq=tps),
        grid=(t // tm,),
        in_specs=[pl.BlockSpec((tm, POOL_W), row),
                  pl.BlockSpec((tm, POOL_W), lambda i: (jnp.maximum(i - 1, 0), 0)),
                  pl.BlockSpec((tm, d), row), pl.BlockSpec((tm, d), row), pl.BlockSpec((tm, d), row),
                  pl.BlockSpec((tm, d), row),
                  pl.BlockSpec((1, N_MOD, d), lambda i: (i // tps, 0, 0)),
                  pl.BlockSpec(w_pool.shape, c3, pipeline_mode=ONE_BUFFER),
                  pl.BlockSpec((1, d), c2),
                  pl.BlockSpec((d, d), c2, pipeline_mode=ONE_BUFFER),
                  pl.BlockSpec((1, d), c2)],
        out_specs=(pl.BlockSpec((tm, d), row), pl.BlockSpec((d, tm), lambda i: (0, i))),
        out_shape=(jax.ShapeDtypeStruct((t, d), F32), jax.ShapeDtypeStruct((d, t), BF16)),
        scratch_shapes=[pltpu.VMEM((tm, d), BF16)],
        compiler_params=_params(("parallel",)),
        name="mix",
    )(u, u, ga, gb, yb, x2d, mod3, w_pool, pool_scale, w_out, norm2_g)


RANK_CODE_BASE = -(2.0 ** 127)


def _select16(s, first_only):
    n = s.shape[0]
    idx = lax.broadcasted_iota(jnp.int32, s.shape, 0) if first_only else None
    cur = s
    rows = []
    for r in range(PEER_TOPK):
        m = jnp.max(cur, axis=0, keepdims=True)
        rows.append(m)
        hit = cur == m
        if first_only:
            hit = idx == jnp.min(jnp.where(hit, idx, n), axis=0, keepdims=True)
        cur = jnp.where(hit, RANK_CODE_BASE * (1.0 + r / 64.0), cur)
    return rows, cur


def _top16_rows(s):
    rows, cur = _select16(s, first_only=False)
    removed = jnp.sum(jnp.where(cur <= RANK_CODE_BASE, 1.0, 0.0), axis=0, keepdims=True)
    tied = jnp.max(removed) > float(PEER_TOPK)
    rows, cur = lax.cond(tied, lambda: _select16(s, first_only=True), lambda: (rows, cur))
    bits = lax.bitcast_convert_type(cur, jnp.int32)
    coded = ((bits >> 17) & 63).astype(F32)
    rank = jnp.where(cur <= RANK_CODE_BASE, coded, float(PEER_TOPK))
    return rows, rank


def _stack_rows(rows, tm):
    n = len(rows)
    rid = lax.broadcasted_iota(jnp.int32, (n, tm), 0)
    arr = jnp.zeros((n, tm), F32)
    for r, v in enumerate(rows):
        arr = jnp.where(rid == r, v, arr)
    return arr


def _candidates(rows1, arr1_hi, rows2, arr2, combine):
    pieces = [combine(rows1[0], arr2)]
    for a in range(1, 8):
        pieces.append(combine(rows1[a], arr2[0:8]))
    pieces.append(combine(arr1_hi, rows2[0]))
    return jnp.concatenate(pieces, axis=0)


def _dup_bf16(x):
    bits = lax.bitcast_convert_type(x.astype(BF16).astype(F32), jnp.uint32)
    return bits | (bits >> 16)


def _row_tile_bf16(ref, h, row, tm):
    words = jnp.broadcast_to(ref[h, pl.ds(row, 1), :], (8, tm))
    return pltpu.bitcast(words, BF16)


def _peer_stats(s1, s2, e1_ref, lim_ref, e2_ref, r2_ref, h):
    tm = s1.shape[1]
    t1, rank1 = _top16_rows(s1)
    t2, rank2 = _top16_rows(s2)
    t1_hi = _stack_rows(t1[8:], tm)
    t2_arr = _stack_rows(t2, tm)
    cand = _candidates(t1, t1_hi, t2, t2_arr, lambda a, b: a + b)
    n_c = cand.shape[0]
    cid = lax.broadcasted_iota(jnp.int32, (n_c, tm), 0)
    cur = cand
    z = jnp.zeros((1, tm), F32)
    m0 = None
    for r in range(PEER_TOPK):
        m = jnp.max(cur, axis=0, keepdims=True)
        if r == 0:
            m0 = m
        first = jnp.min(jnp.where(cur == m, cid, n_c), axis=0, keepdims=True)
        cur = jnp.where(cid == first, -jnp.inf, cur)
        z = z + jnp.exp(m - m0)
    sel = jnp.where(cur == -jnp.inf, 1.0, 0.0)
    width = [jnp.sum(sel[0:16], axis=0, keepdims=True)]
    for a in range(1, 8):
        width.append(jnp.sum(sel[8 + 8 * a:16 + 8 * a], axis=0, keepdims=True))
    for a in range(8, PEER_TOPK):
        width.append(sel[64 + a:65 + a])
    lim = jnp.zeros_like(s1)
    for a in range(PEER_TOPK):
        lim = jnp.where(rank1 == float(a), width[a], lim)
    inv_z = 1.0 / z
    e1_ref[h] = _dup_bf16(jnp.exp(s1 - t1[0]) * inv_z)
    lim_ref[h] = _dup_bf16(lim)
    e2_ref[h] = jnp.exp(s2 - t2[0]).astype(BF16)
    r2_ref[h] = rank2.astype(BF16)


def _peer_kernel(wq_ref, sk_ref, h2_ref, u_ref, vt_ref, x1_ref, mod_ref, fg_ref, o_ref,
                 e1_ref, lim_ref, e2_ref, r2_ref, acc_ref, y_ref):
    s = pl.program_id(1)
    n_heads = e1_ref.shape[0]
    eb = u_ref.shape[0]
    tm = h2_ref.shape[1]
    rows_per_blk = eb // N_KEYS
    kc = 2 * N_KEYS

    @pl.when(s == 0)
    def _():
        half = sk_ref.shape[2]

        def scores(h):
            r0 = pl.multiple_of(h * 2 * half, 2 * half)
            qp = _dot(wq_ref[pl.ds(r0, 2 * half), :], h2_ref[...]).astype(BF16)
            return _dot(sk_ref[2 * h], qp[:half]), _dot(sk_ref[2 * h + 1], qp[half:])

        def head(h, carry):
            s1, s2 = carry
            nxt = scores(jnp.minimum(h + 1, n_heads - 1))
            _peer_stats(s1, s2, e1_ref, lim_ref, e2_ref, r2_ref, h)
            return nxt
        lax.fori_loop(0, n_heads, head, scores(0))
        acc_ref[...] = jnp.zeros_like(acc_ref)

    def expert_block():
        a_chunks = [_dot(u_ref[c * kc:(c + 1) * kc, :], h2_ref[...]) for c in range(eb // kc)]
        for ii in range(rows_per_blk):
            i_key = s * rows_per_blk + ii
            sub = BF16_ROWS
            g = None
            for h in range(n_heads):
                p = _row_tile_bf16(e1_ref, h, i_key, tm)[None] * e2_ref[h].reshape(N_KEYS // sub, sub, tm)
                lim = _row_tile_bf16(lim_ref, h, i_key, tm)[None]
                term = jnp.where(r2_ref[h].reshape(N_KEYS // sub, sub, tm) < lim, p, jnp.zeros_like(p))
                g = term if g is None else g + term
            g = g.reshape(N_KEYS, tm)
            r_in = ii * N_KEYS % kc
            a = a_chunks[ii * N_KEYS // kc][r_in:r_in + N_KEYS].astype(BF16)
            act = (0.5 * a) * (1.0 + lax.erf(a * (0.5 ** 0.5)))
            y_ref[ii * N_KEYS:(ii + 1) * N_KEYS, :] = act * g
            if (ii + 1) * N_KEYS % kc == 0:
                c0 = (ii + 1) * N_KEYS - kc
                acc_ref[...] += _dot(vt_ref[:, c0:c0 + kc], y_ref[c0:c0 + kc, :])

    expert_block()

    @pl.when(s == pl.num_programs(1) - 1)
    def _():
        peer = acc_ref[...].T
        x2 = x1_ref[...] + mod_ref[0, 5:6, :] * peer
        ms = jnp.mean(x2 * x2, axis=-1, keepdims=True)
        o_ref[...] = x2 * lax.rsqrt(ms + EPS) * fg_ref[...]


def _peer_call(wq_t, sub_keys, h2, u_bf, vt_bf, x1, mod3, final_g, seq):
    t, d = x1.shape
    tm, eb = PEER_TM, PEER_EB
    tps = seq // tm
    n_exp = u_bf.shape[0]
    row = lambda i, j: (i, 0)
    return pl.pallas_call(
        _peer_kernel,
        grid=(t // tm, n_exp // eb),
        in_specs=[pl.BlockSpec(wq_t.shape, lambda i, j: (0, 0), pipeline_mode=ONE_BUFFER),
                  pl.BlockSpec(sub_keys.shape, lambda i, j: (0, 0, 0), pipeline_mode=ONE_BUFFER),
                  pl.BlockSpec((d, tm), lambda i, j: (0, i)),
                  pl.BlockSpec((eb, d), lambda i, j: (j, 0)),
                  pl.BlockSpec((d, eb), lambda i, j: (0, j)),
                  pl.BlockSpec((tm, d), row),
                  pl.BlockSpec((1, N_MOD, d), lambda i, j: (i // tps, 0, 0)),
                  pl.BlockSpec((1, d), lambda i, j: (0, 0))],
        out_specs=pl.BlockSpec((tm, d), row),
        out_shape=jax.ShapeDtypeStruct((t, d), F32),
        scratch_shapes=[pltpu.VMEM((PEER_HEADS, N_KEYS, tm), jnp.uint32),
                        pltpu.VMEM((PEER_HEADS, N_KEYS, tm), jnp.uint32),
                        pltpu.VMEM((PEER_HEADS, N_KEYS, tm), BF16),
                        pltpu.VMEM((PEER_HEADS, N_KEYS, tm), BF16),
                        pltpu.VMEM((d, tm), F32),
                        pltpu.VMEM((eb, tm), BF16)],
        compiler_params=_params(("parallel", "arbitrary")),
        name="peer",
    )(wq_t, sub_keys, h2, u_bf, vt_bf, x1, mod3, final_g)


def kernel(x, c, w_mod, b_mod, norm1_g, w_in, b_f, w_pool, pool_scale, w_out, norm2_g,
           peer_w_query, peer_sub_keys, peer_u, peer_v, final_g):
    bsz, seq, d = x.shape
    depth = w_mod.shape[0]
    t = bsz * seq
    assert d == ATTN_HEADS * HEAD_DIM and seq % INPROJ_TM == 0 and t % PEER_TM == 0
    x2d = x.reshape(t, d)
    out = x2d
    for l in range(depth):
        mod3 = _mod_call(c, w_mod[l], b_mod[l]).reshape(bsz, N_MOD, d)

        w = w_in[l]
        o_q, o_k, o_v = POOL_W, POOL_W + d, POOL_W + 2 * d
        o_f = POOL_W + 3 * d
        o_g = o_f + ATTN_HEADS
        w_tok = jnp.concatenate([w[:, :o_q], w[:, o_k:o_v], w[:, o_g:]], axis=1).astype(BF16)
        w_feat_t = jnp.concatenate([w[:, o_q:o_k], w[:, o_v:o_f]], axis=1).T.astype(BF16)
        w_f = jnp.pad(w[:, o_f:o_g], ((0, 0), (0, LANES - ATTN_HEADS))).astype(BF16)
        b_f_row = jnp.pad(b_f[l], (0, LANES - ATTN_HEADS)).reshape(1, LANES)
        u, k, fa, ga, gb, qt, vt = _inproj_call(
            x2d, mod3, norm1_g[l].reshape(1, d), w_tok, w_feat_t, w_f, b_f_row,
            _forget_selectors(), seq)

        yb = _attn_call(qt, k, fa, vt, bsz, seq)

        n_hp = 2 * PEER_HEADS
        x1, h2 = _mix_call(
            u, ga, gb, yb, x2d, mod3, w_pool[l].astype(BF16), pool_scale[l].reshape(1, d),
            w_out[l].astype(BF16), norm2_g[l].reshape(1, d), seq)

        assert l == depth - 1 == 0
        out = _peer_call(peer_w_query[l].T.astype(BF16),
                         peer_sub_keys[l].reshape(n_hp, N_KEYS, -1).astype(BF16), h2, peer_u[l].astype(BF16), peer_v[l].T.astype(BF16),
                         x1, mod3, final_g.reshape(1, d), seq)
        x2d = out
    return out.reshape(bsz, seq, d)
```
